```python
import jax, jax.numpy as jnp
from jax import lax
import numpy as np

D_MODEL = 1024
BATCH = 4
SEQ = 4096
DEPTH = 2
DEC_BATCH = 32
DEC_SEQ = 4
PAST_LEN = 8192
PAGE_SIZE = 128

N_META = 16
N_GROUPS = 4
GROUP_W = D_MODEL // N_GROUPS
MIX_W = N_GROUPS * GROUP_W
CONV_A_WIDTH = 31
CONV_B_WIDTH = 3
N_HEADS = 4
HEAD_DIM = GROUP_W // N_HEADS
N_IDX_HEADS = 8
IDX_DIM = 64
TOPK_MAX = 256
Q_BLOCK = 128
POOL_WINDOWS = (2, 4, 8, 16)
POOL_GROUP = GROUP_W // len(POOL_WINDOWS)
POOL_CTX = max(POOL_WINDOWS) - 1
PEER_HEADS = 8
N_KEYS = 128
N_EXPERTS = N_KEYS * N_KEYS
PEER_QDIM = 256
PEER_TOPK = 16
TOKEN_BLOCK = 128
RMS_EPS = 1e-6
LN_EPS = 1e-5
SPLIT_SIZES = (GROUP_W, GROUP_W, GROUP_W, GROUP_W, GROUP_W, GROUP_W, GROUP_W, GROUP_W,
               N_IDX_HEADS * IDX_DIM, IDX_DIM, N_IDX_HEADS, GROUP_W)
N_COLS = sum(SPLIT_SIZES)

kernel_name = 'hybrid_conformer_shortconv_dsa_pool_peer_step'


def split_columns(z):
    parts, start = [], 0
    for size in SPLIT_SIZES:
        parts.append(z[..., start:start + size])
        start += size
    return parts


def rms_norm(x, g):
    xf = x.astype(jnp.float32)
    y = xf * lax.rsqrt(jnp.mean(xf * xf, axis=-1, keepdims=True) + RMS_EPS)
    return (y * g.astype(jnp.float32)).astype(x.dtype)


def layer_norm(x, g, b):
    xf = x.astype(jnp.float32)
    mu = jnp.mean(xf, axis=-1, keepdims=True)
    var = jnp.mean(jnp.square(xf - mu), axis=-1, keepdims=True)
    y = (xf - mu) * lax.rsqrt(var + LN_EPS)
    return (y * g.astype(jnp.float32) + b.astype(jnp.float32)).astype(x.dtype)


def causal_dwconv(x_ext, w):
    c = x_ext.shape[-1]
    return lax.conv_general_dilated(x_ext, w[:, None, :].astype(x_ext.dtype), window_strides=(1,),
                                    padding='VALID', dimension_numbers=('NWC', 'WIO', 'NWC'),
                                    feature_group_count=c)


def multiscale_pool(x_ext, n_ctx, start_pos):
    b, l, _ = x_ext.shape
    t = l - n_ctx
    xf = x_ext.astype(jnp.float32)
    cs = jnp.concatenate([jnp.zeros((b, 1, GROUP_W), jnp.float32), jnp.cumsum(xf, axis=1)], axis=1)
    rows = jnp.arange(n_ctx, l)
    pos = start_pos + jnp.arange(t)
    outs = []
    for gi, w in enumerate(POOL_WINDOWS):
        csg = cs[..., gi * POOL_GROUP:(gi + 1) * POOL_GROUP]
        lo = jnp.maximum(rows + 1 - w, 0)
        cnt = jnp.minimum(pos + 1, w).astype(jnp.float32)
        outs.append((csg[:, rows + 1] - csg[:, lo]) / cnt[None, :, None])
    mean = jnp.concatenate(outs, axis=-1)
    return (mean - xf[:, n_ctx:]).astype(x_ext.dtype)


def indexer_topk(q_idx, w_idx, k_idx, q_pos, topk):
    s = jnp.einsum('bqhd,bld->bqhl', q_idx, k_idx, preferred_element_type=jnp.float32)
    score = jnp.einsum('bqh,bqhl->bql', w_idx.astype(jnp.float32), jax.nn.relu(s))
    visible = jnp.arange(k_idx.shape[1])[None, :] <= q_pos[:, None]
    score = jnp.where(visible[None], score, -jnp.inf)
    top_score, top_idx = lax.top_k(score, topk)
    return top_idx, top_score > -jnp.inf


def attend_selected(q, k_sel, v_sel, valid):
    logits = jnp.einsum('bqhd,bqkhd->bqhk', q, k_sel, preferred_element_type=jnp.float32) * (HEAD_DIM ** -0.5)
    logits = jnp.where(valid[:, :, None, :], logits, -jnp.inf)
    p = jax.nn.softmax(logits, axis=-1)
    return jnp.einsum('bqhk,bqkhd->bqhd', p.astype(v_sel.dtype), v_sel)


def take_rows(a, idx):
    return jax.vmap(lambda arr, i: arr[i])(a, idx)


def dsa_prompt(q, k, v, q_idx, k_idx, w_idx):
    b, t = q.shape[:2]
    topk = min(TOPK_MAX, t // 4)
    n_blk = -(-t // Q_BLOCK)
    pad = n_blk * Q_BLOCK - t

    def blocks(a):
        a = jnp.pad(a, [(0, 0), (0, pad)] + [(0, 0)] * (a.ndim - 2))
        return a.reshape((b, n_blk, Q_BLOCK) + a.shape[2:]).swapaxes(0, 1)

    def one_block(args):
        blk, qb, qib, wb = args
        q_pos = blk * Q_BLOCK + jnp.arange(Q_BLOCK)
        top_idx, valid = indexer_topk(qib, wb, k_idx, q_pos, topk)
        return attend_selected(qb, take_rows(k, top_idx), take_rows(v, top_idx), valid)

    out = lax.map(one_block, (jnp.arange(n_blk), blocks(q), blocks(q_idx), blocks(w_idx)))
    return out.swapaxes(0, 1).reshape(b, n_blk * Q_BLOCK, N_HEADS, HEAD_DIM)[:, :t]


def dsa_sample(q, k, v, q_idx, k_idx, w_idx, k_pool, v_pool, kidx_pool, page_table):
    db, t = q.shape[:2]
    n_pages = page_table.shape[1]
    past_len = n_pages * PAGE_SIZE
    topk = min(TOPK_MAX, (past_len + t) // 4)
    kidx_past = kidx_pool[page_table].reshape(db, past_len, IDX_DIM)
    kidx_all = jnp.concatenate([kidx_past.astype(k_idx.dtype), k_idx], axis=1)
    q_pos = past_len + jnp.arange(t)
    top_idx, valid = indexer_topk(q_idx, w_idx, kidx_all, q_pos, topk)
    is_past = (top_idx < past_len)[..., None, None]
    phys = take_rows(page_table, jnp.minimum(top_idx // PAGE_SIZE, n_pages - 1))
    off = top_idx % PAGE_SIZE
    new_i = jnp.clip(top_idx - past_len, 0, t - 1)
    k_sel = jnp.where(is_past, k_pool[phys, off].astype(k.dtype), take_rows(k, new_i))
    v_sel = jnp.where(is_past, v_pool[phys, off].astype(v.dtype), take_rows(v, new_i))
    return attend_selected(q, k_sel, v_sel, valid)


def mixer_block(hn, lw, past, start_pos):
    b, t, _ = hn.shape
    (a_val, a_gate, b_gate, c_gate, b_in, q, k, v,
     q_idx, k_idx, w_idx, pool_in) = split_columns(hn @ lw['w_in'])
    u = a_val * jax.nn.sigmoid(a_gate)
    ctx_a = jnp.zeros((b, CONV_A_WIDTH - 1, GROUP_W), u.dtype) if past is None else past['conv_a'].astype(u.dtype)
    u_ext = jnp.concatenate([ctx_a, u], axis=1)
    y_a = causal_dwconv(u_ext, lw['conv_a_w']) + lw['conv_a_b']
    y_a = jax.nn.silu(layer_norm(y_a, lw['ln_a_g'], lw['ln_a_b']))
    g_in = c_gate * b_in
    ctx_b = jnp.zeros((b, CONV_B_WIDTH - 1, GROUP_W), g_in.dtype) if past is None else past['conv_b'].astype(g_in.dtype)
    g_ext = jnp.concatenate([ctx_b, g_in], axis=1)
    y_b = b_gate * causal_dwconv(g_ext, lw['conv_b_w'])
    q = q.reshape(b, t, N_HEADS, HEAD_DIM)
    k = k.reshape(b, t, N_HEADS, HEAD_DIM)
    v = v.reshape(b, t, N_HEADS, HEAD_DIM)
    q_idx = q_idx.reshape(b, t, N_IDX_HEADS, IDX_DIM)
    if past is None:
        y_c = dsa_prompt(q, k, v, q_idx, k_idx, w_idx)
    else:
        y_c = dsa_sample(q, k, v, q_idx, k_idx, w_idx, past['k_pool'], past['v_pool'],
                         past['kidx_pool'], past['page_table'])
    y_c = y_c.reshape(b, t, GROUP_W)
    if past is None:
        p_ext, n_ctx = pool_in, 0
    else:
        p_ext, n_ctx = jnp.concatenate([past['pool'].astype(pool_in.dtype), pool_in], axis=1), POOL_CTX
    pooled = multiscale_pool(p_ext, n_ctx, start_pos).reshape(b, t, len(POOL_WINDOWS), POOL_GROUP)
    y_d = jnp.einsum('btgc,gcd->btgd', pooled, lw['pool_w']).reshape(b, t, GROUP_W) * lw['pool_scale']
    y = jnp.concatenate([y_a, y_b, y_c, y_d], axis=-1) @ lw['w_out']
    new = (k, v, k_idx, u_ext[:, -(CONV_A_WIDTH - 1):], g_ext[:, -(CONV_B_WIDTH - 1):], p_ext[:, -POOL_CTX:])
    return y, new


def peer_ffn(hn, lw):
    b, t, d = hn.shape
    n = b * t
    n_blk = -(-n // TOKEN_BLOCK)
    xb = jnp.pad(hn.reshape(n, d), [(0, n_blk * TOKEN_BLOCK - n), (0, 0)]).reshape(n_blk, TOKEN_BLOCK, d)
    half = PEER_QDIM // 2

    def one_block(xblk):
        qh = (xblk @ lw['peer_wq']).reshape(TOKEN_BLOCK, PEER_HEADS, 2, half)
        s = jnp.einsum('nhpc,hpkc->nhpk', qh, lw['peer_subkeys'], preferred_element_type=jnp.float32)
        v_top, i_top = lax.top_k(s, PEER_TOPK)
        cand = v_top[:, :, 0, :, None] + v_top[:, :, 1, None, :]
        cand_idx = i_top[:, :, 0, :, None] * N_KEYS + i_top[:, :, 1, None, :]
        best, sel = lax.top_k(cand.reshape(TOKEN_BLOCK, PEER_HEADS, -1), PEER_TOPK)
        experts = jnp.take_along_axis(cand_idx.reshape(TOKEN_BLOCK, PEER_HEADS, -1), sel, axis=-1)
        gate = jax.nn.softmax(best, axis=-1)
        act = jnp.einsum('nd,nhkd->nhk', xblk, lw['peer_u'][experts], preferred_element_type=jnp.float32)
        coef = (gate * jax.nn.gelu(act)).astype(xblk.dtype)
        return jnp.einsum('nhk,nhkd->nd', coef, lw['peer_v'][experts])

    out = lax.map(one_block, xb).reshape(n_blk * TOKEN_BLOCK, d)[:n]
    return out.reshape(b, t, d)


def setup_inputs(seed: int = 0) -> dict:
    key = jax.random.key(seed)
    ks = jax.random.split(key, 26)
    n_pages = PAST_LEN // PAGE_SIZE
    n_pool = (DEC_BATCH * n_pages * 5) // 4

    def nrm(k, shape, scale=1.0):
        return scale * jax.random.normal(k, shape, jnp.float32)

    page_table = jax.random.permutation(ks[8], n_pool)[:DEC_BATCH * n_pages].reshape(DEC_BATCH, n_pages).astype(jnp.int32)
    return {
        'x_prompt': nrm(ks[0], (BATCH, SEQ, D_MODEL)),
        'x_sample': nrm(ks[1], (DEC_BATCH, DEC_SEQ, D_MODEL)),
        'cache_k': nrm(ks[2], (DEPTH, n_pool, PAGE_SIZE, N_HEADS, HEAD_DIM)),
        'cache_v': nrm(ks[3], (DEPTH, n_pool, PAGE_SIZE, N_HEADS, HEAD_DIM)),
        'cache_kidx': nrm(ks[4], (DEPTH, n_pool, PAGE_SIZE, IDX_DIM)),
        'state_conv_a': nrm(ks[5], (DEPTH, DEC_BATCH, CONV_A_WIDTH - 1, GROUP_W)),
        'state_conv_b': nrm(ks[6], (DEPTH, DEC_BATCH, CONV_B_WIDTH - 1, GROUP_W)),
        'state_pool': nrm(ks[7], (DEPTH, DEC_BATCH, POOL_CTX, GROUP_W)),
        'page_table': page_table,
        'meta_tokens': nrm(ks[9], (N_META, D_MODEL)),
        'w_in': nrm(ks[10], (DEPTH, D_MODEL, N_COLS), D_MODEL ** -0.5),
        'conv_a_w': nrm(ks[11], (DEPTH, CONV_A_WIDTH, GROUP_W), CONV_A_WIDTH ** -0.5),
        'conv_a_b': nrm(ks[12], (DEPTH, GROUP_W), 0.02),
        'ln_a_g': 1.0 + nrm(ks[13], (DEPTH, GROUP_W), 0.02),
        'ln_a_b': nrm(ks[14], (DEPTH, GROUP_W), 0.02),
        'conv_b_w': nrm(ks[15], (DEPTH, CONV_B_WIDTH, GROUP_W), CONV_B_WIDTH ** -0.5),
        'pool_w': nrm(ks[16], (DEPTH, len(POOL_WINDOWS), POOL_GROUP, POOL_GROUP), POOL_GROUP ** -0.5),
        'pool_scale': 1.0 + nrm(ks[17], (DEPTH, GROUP_W), 0.02),
        'w_out': nrm(ks[18], (DEPTH, MIX_W, D_MODEL), MIX_W ** -0.5),
        'norm_mix_g': 1.0 + nrm(ks[19], (DEPTH, D_MODEL), 0.02),
        'norm_ffn_g': 1.0 + nrm(ks[20], (DEPTH, D_MODEL), 0.02),
        'peer_wq': nrm(ks[21], (DEPTH, D_MODEL, PEER_HEADS * PEER_QDIM), D_MODEL ** -0.5),
        'peer_subkeys': nrm(ks[22], (DEPTH, PEER_HEADS, 2, N_KEYS, PEER_QDIM // 2), (PEER_QDIM // 2) ** -0.5),
        'peer_u': nrm(ks[23], (DEPTH, N_EXPERTS, D_MODEL), D_MODEL ** -0.5),
        'peer_v': nrm(ks[24], (DEPTH, N_EXPERTS, D_MODEL), PEER_HEADS ** -0.5),
        'norm_final_g': 1.0 + nrm(ks[25], (D_MODEL,), 0.02),
    }


def reference(x_prompt, x_sample, cache_k, cache_v, cache_kidx, state_conv_a, state_conv_b, state_pool,
              page_table, meta_tokens, w_in, conv_a_w, conv_a_b, ln_a_g, ln_a_b, conv_b_w, pool_w,
              pool_scale, w_out, norm_mix_g, norm_ffn_g, peer_wq, peer_subkeys, peer_u, peer_v, norm_final_g):
    b = x_prompt.shape[0]
    meta = jnp.broadcast_to(meta_tokens.astype(x_prompt.dtype)[None], (b, N_META, D_MODEL))
    hp = jnp.concatenate([meta, x_prompt], axis=1)
    hs = x_sample
    past_len = page_table.shape[1] * PAGE_SIZE
    news_p, news_s = [], []
    for l in range(DEPTH):
        lw = {'w_in': w_in[l], 'conv_a_w': conv_a_w[l], 'conv_a_b': conv_a_b[l], 'ln_a_g': ln_a_g[l],
              'ln_a_b': ln_a_b[l], 'conv_b_w': conv_b_w[l], 'pool_w': pool_w[l], 'pool_scale': pool_scale[l],
              'w_out': w_out[l], 'peer_wq': peer_wq[l], 'peer_subkeys': peer_subkeys[l],
              'peer_u': peer_u[l], 'peer_v': peer_v[l]}
        past = {'k_pool': cache_k[l], 'v_pool': cache_v[l], 'kidx_pool': cache_kidx[l],
                'conv_a': state_conv_a[l], 'conv_b': state_conv_b[l], 'pool': state_pool[l],
                'page_table': page_table}
        mp, new_p = mixer_block(rms_norm(hp, norm_mix_g[l]), lw, None, 0)
        ms, new_s = mixer_block(rms_norm(hs, norm_mix_g[l]), lw, past, past_len)
        hp = hp + mp
        hs = hs + ms
        hp = hp + peer_ffn(rms_norm(hp, norm_ffn_g[l]), lw)
        hs = hs + peer_ffn(rms_norm(hs, norm_ffn_g[l]), lw)
        news_p.append(new_p)
        news_s.append(new_s)
    y_prompt = rms_norm(hp, norm_final_g)[:, N_META:]
    y_sample = rms_norm(hs, norm_final_g)
    k_p = jnp.stack([n[0] for n in news_p])
    v_p = jnp.stack([n[1] for n in news_p])
    kidx_p = jnp.stack([n[2] for n in news_p])
    conv_a_p = jnp.stack([n[3] for n in news_p])
    conv_b_p = jnp.stack([n[4] for n in news_p])
    pool_p = jnp.stack([n[5] for n in news_p])
    k_s = jnp.stack([n[0] for n in news_s])
    v_s = jnp.stack([n[1] for n in news_s])
    kidx_s = jnp.stack([n[2] for n in news_s])
    conv_a_s = jnp.stack([n[3] for n in news_s])
    conv_b_s = jnp.stack([n[4] for n in news_s])
    pool_s = jnp.stack([n[5] for n in news_s])
    return (y_prompt, y_sample, k_p, v_p, kidx_p, conv_a_p, conv_b_p, pool_p,
            k_s, v_s, kidx_s, conv_a_s, conv_b_s, pool_s)
```

```python
import functools

import jax
import jax.numpy as jnp
from jax import lax
from jax.experimental import pallas as pl
from jax.experimental.pallas import tpu as pltpu

D_MODEL = 1024
N_META = 16
GROUP_W = 256
CONV_A_WIDTH = 31
CONV_B_WIDTH = 3
N_HEADS = 4
HEAD_DIM = 64
N_IDX_HEADS = 8
IDX_DIM = 64
TOPK_MAX = 256
Q_BLOCK = 128
PAGE_SIZE = 128
POOL_WINDOWS = (2, 4, 8, 16)
POOL_GROUP = 64
POOL_CTX = 15
PEER_HEADS = 8
N_KEYS = 128
PEER_QDIM = 256
PEER_TOPK = 16
TOKEN_BLOCK = 128
RMS_EPS = 1e-6
LN_EPS = 1e-5
SPLIT_SIZES = (GROUP_W,) * 8 + (N_IDX_HEADS * IDX_DIM, IDX_DIM, N_IDX_HEADS, GROUP_W)
N_COLS = sum(SPLIT_SIZES)

VMEM_LIMIT_BYTES = 48 * 1024 * 1024


def _norm_matmul_kernel(x_ref, g_ref, w_ref, o_ref):
    x = x_ref[...]
    y = x * lax.rsqrt(jnp.mean(x * x, axis=-1, keepdims=True) + RMS_EPS) * g_ref[...]
    o_ref[...] = jnp.dot(y, w_ref[...], preferred_element_type=jnp.float32)


def norm_matmul(x, g, w, tm=256):
    n, d = x.shape
    c = w.shape[1]
    n_pad = -(-n // tm) * tm
    xp = jnp.pad(x, [(0, n_pad - n), (0, 0)])
    out = pl.pallas_call(
        _norm_matmul_kernel,
        grid=(n_pad // tm,),
        in_specs=[pl.BlockSpec((tm, d), lambda i: (i, 0)),
                  pl.BlockSpec((1, d), lambda i: (0, 0)),
                  pl.BlockSpec((d, c), lambda i: (0, 0))],
        out_specs=pl.BlockSpec((tm, c), lambda i: (i, 0)),
        out_shape=jax.ShapeDtypeStruct((n_pad, c), jnp.float32),
        compiler_params=pltpu.CompilerParams(vmem_limit_bytes=VMEM_LIMIT_BYTES),
        name="norm_matmul",
    )(xp, g.reshape(1, d), w)
    return out[:n]


def split_columns(z):
    parts, start = [], 0
    for size in SPLIT_SIZES:
        parts.append(z[..., start:start + size])
        start += size
    return parts


def rms_norm(x, g):
    y = x * lax.rsqrt(jnp.mean(x * x, axis=-1, keepdims=True) + RMS_EPS)
    return y * g


def layer_norm(x, g, b):
    mu = jnp.mean(x, axis=-1, keepdims=True)
    var = jnp.mean(jnp.square(x - mu), axis=-1, keepdims=True)
    return (x - mu) * lax.rsqrt(var + LN_EPS) * g + b


def causal_dwconv(x_ext, w):
    c = x_ext.shape[-1]
    return lax.conv_general_dilated(x_ext, w[:, None, :], window_strides=(1,), padding='VALID',
                                    dimension_numbers=('NWC', 'WIO', 'NWC'), feature_group_count=c)


def multiscale_pool(x_ext, n_ctx, start_pos):
    b, l, _ = x_ext.shape
    t = l - n_ctx
    cs = jnp.concatenate([jnp.zeros((b, 1, GROUP_W), jnp.float32), jnp.cumsum(x_ext, axis=1)], axis=1)
    rows = jnp.arange(n_ctx, l)
    pos = start_pos + jnp.arange(t)
    outs = []
    for gi, w in enumerate(POOL_WINDOWS):
        csg = cs[..., gi * POOL_GROUP:(gi + 1) * POOL_GROUP]
        lo = jnp.maximum(rows + 1 - w, 0)
        cnt = jnp.minimum(pos + 1, w).astype(jnp.float32)
        outs.append((csg[:, rows + 1] - csg[:, lo]) / cnt[None, :, None])
    mean = jnp.concatenate(outs, axis=-1)
    return mean - x_ext[:, n_ctx:]


def indexer_topk(q_idx, w_idx, k_idx, q_pos, topk):
    s = jnp.einsum('bqhd,bld->bqhl', q_idx, k_idx, preferred_element_type=jnp.float32)
    score = jnp.einsum('bqh,bqhl->bql', w_idx, jax.nn.relu(s))
    visible = jnp.arange(k_idx.shape[1])[None, :] <= q_pos[:, None]
    score = jnp.where(visible[None], score, -jnp.inf)
    top_score, top_idx = lax.top_k(score, topk)
    return top_idx, top_score > -jnp.inf


def attend_selected(q, k_sel, v_sel, valid):
    logits = jnp.einsum('bqhd,bqkhd->bqhk', q, k_sel, preferred_element_type=jnp.float32) * (HEAD_DIM ** -0.5)
    logits = jnp.where(valid[:, :, None, :], logits, -jnp.inf)
    p = jax.nn.softmax(logits, axis=-1)
    return jnp.einsum('bqhk,bqkhd->bqhd', p, v_sel)


def take_rows(a, idx):
    return jax.vmap(lambda arr, i: arr[i])(a, idx)


def dsa_prompt(q, k, v, q_idx, k_idx, w_idx):
    b, t = q.shape[:2]
    topk = min(TOPK_MAX, t // 4)
    n_blk = -(-t // Q_BLOCK)
    pad = n_blk * Q_BLOCK - t

    def blocks(a):
        a = jnp.pad(a, [(0, 0), (0, pad)] + [(0, 0)] * (a.ndim - 2))
        return a.reshape((b, n_blk, Q_BLOCK) + a.shape[2:]).swapaxes(0, 1)

    def one_block(args):
        blk, qb, qib, wb = args
        q_pos = blk * Q_BLOCK + jnp.arange(Q_BLOCK)
        top_idx, valid = indexer_topk(qib, wb, k_idx, q_pos, topk)
        return attend_selected(qb, take_rows(k, top_idx), take_rows(v, top_idx), valid)

    out = lax.map(one_block, (jnp.arange(n_blk), blocks(q), blocks(q_idx), blocks(w_idx)))
    return out.swapaxes(0, 1).reshape(b, n_blk * Q_BLOCK, N_HEADS, HEAD_DIM)[:, :t]


def dsa_sample(q, k, v, q_idx, k_idx, w_idx, k_pool, v_pool, kidx_pool, page_table):
    db, t = q.shape[:2]
    n_pages = page_table.shape[1]
    past_len = n_pages * PAGE_SIZE
    topk = min(TOPK_MAX, (past_len + t) // 4)
    kidx_past = kidx_pool[page_table].reshape(db, past_len, IDX_DIM)
    kidx_all = jnp.concatenate([kidx_past, k_idx], axis=1)
    q_pos = past_len + jnp.arange(t)
    top_idx, valid = indexer_topk(q_idx, w_idx, kidx_all, q_pos, topk)
    is_past = (top_idx < past_len)[..., None, None]
    phys = take_rows(page_table, jnp.minimum(top_idx // PAGE_SIZE, n_pages - 1))
    off = top_idx % PAGE_SIZE
    new_i = jnp.clip(top_idx - past_len, 0, t - 1)
    k_sel = jnp.where(is_past, k_pool[phys, off], take_rows(k, new_i))
    v_sel = jnp.where(is_past, v_pool[phys, off], take_rows(v, new_i))
    return attend_selected(q, k_sel, v_sel, valid)


def mixer_block(h, lw, past, start_pos):
    b, t, _ = h.shape
    z = norm_matmul(h.reshape(b * t, D_MODEL), lw['norm_mix_g'], lw['w_in']).reshape(b, t, N_COLS)
    (a_val, a_gate, b_gate, c_gate, b_in, q, k, v, q_idx, k_idx, w_idx, pool_in) = split_columns(z)
    u = a_val * jax.nn.sigmoid(a_gate)
    ctx_a = jnp.zeros((b, CONV_A_WIDTH - 1, GROUP_W), u.dtype) if past is None else past['conv_a']
    u_ext = jnp.concatenate([ctx_a, u], axis=1)
    y_a = causal_dwconv(u_ext, lw['conv_a_w']) + lw['conv_a_b']
    y_a = jax.nn.silu(layer_norm(y_a, lw['ln_a_g'], lw['ln_a_b']))
    g_in = c_gate * b_in
    ctx_b = jnp.zeros((b, CONV_B_WIDTH - 1, GROUP_W), g_in.dtype) if past is None else past['conv_b']
    g_ext = jnp.concatenate([ctx_b, g_in], axis=1)
    y_b = b_gate * causal_dwconv(g_ext, lw['conv_b_w'])
    q = q.reshape(b, t, N_HEADS, HEAD_DIM)
    k = k.reshape(b, t, N_HEADS, HEAD_DIM)
    v = v.reshape(b, t, N_HEADS, HEAD_DIM)
    q_idx = q_idx.reshape(b, t, N_IDX_HEADS, IDX_DIM)
    if past is None:
        y_c = dsa_prompt(q, k, v, q_idx, k_idx, w_idx)
    else:
        y_c = dsa_sample(q, k, v, q_idx, k_idx, w_idx, past['k_pool'], past['v_pool'],
                         past['kidx_pool'], past['page_table'])
    y_c = y_c.reshape(b, t, GROUP_W)
    if past is None:
        p_ext, n_ctx = pool_in, 0
    else:
        p_ext, n_ctx = jnp.concatenate([past['pool'], pool_in], axis=1), POOL_CTX
    pooled = multiscale_pool(p_ext, n_ctx, start_pos).reshape(b, t, len(POOL_WINDOWS), POOL_GROUP)
    y_d = jnp.einsum('btgc,gcd->btgd', pooled, lw['pool_w']).reshape(b, t, GROUP_W) * lw['pool_scale']
    y = jnp.concatenate([y_a, y_b, y_c, y_d], axis=-1) @ lw['w_out']
    new = (k, v, k_idx, u_ext[:, -(CONV_A_WIDTH - 1):], g_ext[:, -(CONV_B_WIDTH - 1):], p_ext[:, -POOL_CTX:])
    return y, new


def peer_ffn(hn, lw):
    b, t, d = hn.shape
    n = b * t
    n_blk = -(-n // TOKEN_BLOCK)
    xb = jnp.pad(hn.reshape(n, d), [(0, n_blk * TOKEN_BLOCK - n), (0, 0)]).reshape(n_blk, TOKEN_BLOCK, d)
    half = PEER_QDIM // 2

    def one_block(xblk):
        qh = (xblk @ lw['peer_wq']).reshape(TOKEN_BLOCK, PEER_HEADS, 2, half)
        s = jnp.einsum('nhpc,hpkc->nhpk', qh, lw['peer_subkeys'], preferred_element_type=jnp.float32)
        v_top, i_top = lax.top_k(s, PEER_TOPK)
        cand = v_top[:, :, 0, :, None] + v_top[:, :, 1, None, :]
        cand_idx = i_top[:, :, 0, :, None] * N_KEYS + i_top[:, :, 1, None, :]
        best, sel = lax.top_k(cand.reshape(TOKEN_BLOCK, PEER_HEADS, -1), PEER_TOPK)
        experts = jnp.take_along_axis(cand_idx.reshape(TOKEN_BLOCK, PEER_HEADS, -1), sel, axis=-1)
        gate = jax.nn.softmax(best, axis=-1)
        act = jnp.einsum('nd,nhkd->nhk', xblk, lw['peer_u'][experts], preferred_element_type=jnp.float32)
        coef = gate * jax.nn.gelu(act)
        return jnp.einsum('nhk,nhkd->nd', coef, lw['peer_v'][experts])

    out = lax.map(one_block, xb).reshape(n_blk * TOKEN_BLOCK, d)[:n]
    return out.reshape(b, t, d)


def kernel(x_prompt, x_sample, cache_k, cache_v, cache_kidx, state_conv_a, state_conv_b, state_pool,
           page_table, meta_tokens, w_in, conv_a_w, conv_a_b, ln_a_g, ln_a_b, conv_b_w, pool_w,
           pool_scale, w_out, norm_mix_g, norm_ffn_g, peer_wq, peer_subkeys, peer_u, peer_v, norm_final_g):
    b = x_prompt.shape[0]
    depth = w_in.shape[0]
    meta = jnp.broadcast_to(meta_tokens[None], (b, N_META, D_MODEL))
    hp = jnp.concatenate([meta, x_prompt], axis=1)
    hs = x_sample
    past_len = page_table.shape[1] * PAGE_SIZE
    news_p, news_s = [], []
    for l in range(depth):
        lw = {'w_in': w_in[l], 'conv_a_w': conv_a_w[l], 'conv_a_b': conv_a_b[l], 'ln_a_g': ln_a_g[l],
              'ln_a_b': ln_a_b[l], 'conv_b_w': conv_b_w[l], 'pool_w': pool_w[l], 'pool_scale': pool_scale[l],
              'w_out': w_out[l], 'peer_wq': peer_wq[l], 'peer_subkeys': peer_subkeys[l],
              'peer_u': peer_u[l], 'peer_v': peer_v[l], 'norm_mix_g': norm_mix_g[l]}
        past = {'k_pool': cache_k[l], 'v_pool': cache_v[l], 'kidx_pool': cache_kidx[l],
                'conv_a': state_conv_a[l], 'conv_b': state_conv_b[l], 'pool': state_pool[l],
                'page_table': page_table}
        mp, new_p = mixer_block(hp, lw, None, 0)
        ms, new_s = mixer_block(hs, lw, past, past_len)
        hp = hp + mp
        hs = hs + ms
        hp = hp + peer_ffn(rms_norm(hp, norm_ffn_g[l]), lw)
        hs = hs + peer_ffn(rms_norm(hs, norm_ffn_g[l]), lw)
        news_p.append(new_p)
        news_s.append(new_s)
    y_prompt = rms_norm(hp, norm_final_g)[:, N_META:]
    y_sample = rms_norm(hs, norm_final_g)
    outs_p = [jnp.stack([n[i] for n in news_p]) for i in range(6)]
    outs_s = [jnp.stack([n[i] for n in news_s]) for i in range(6)]
    return (y_prompt, y_sample, *outs_p, *outs_s)
```

```python
import functools

import jax
import jax.numpy as jnp
from jax import lax
from jax.experimental import pallas as pl
from jax.experimental.pallas import tpu as pltpu

D_MODEL = 1024
N_META = 16
GROUP_W = 256
CONV_A_WIDTH = 31
CONV_B_WIDTH = 3
N_HEADS = 4
HEAD_DIM = 64
N_IDX_HEADS = 8
IDX_DIM = 64
TOPK_MAX = 256
PAGE_SIZE = 128
POOL_WINDOWS = (2, 4, 8, 16)
POOL_GROUP = 64
POOL_CTX = 15
PEER_HEADS = 8
N_KEYS = 128
PEER_HALF = 128
PEER_TOPK = 16
N_ROUTES = PEER_HEADS * PEER_TOPK
RMS_EPS = 1e-6
LN_EPS = 1e-5
SPLIT_SIZES = (GROUP_W,) * 8 + (N_IDX_HEADS * IDX_DIM, IDX_DIM, N_IDX_HEADS, GROUP_W)
N_COLS = sum(SPLIT_SIZES)

Q_BLOCK = 128
KEY_CHUNK = 256
ROW_TILE = 256
ROUTE_TILE = 128
PEER_TILE = 256
KEYS_PER_STEP = 8
INT_MIN = -2 ** 31
NEG_INF_KEY = -2139095041
VMEM_LIMIT_BYTES = 48 * 1024 * 1024


def _compiler_params(**kw):
    return pltpu.CompilerParams(vmem_limit_bytes=VMEM_LIMIT_BYTES, **kw)


def _norm_matmul_kernel(x_ref, g_ref, w_ref, o_ref):
    x = x_ref[...]
    y = x * lax.rsqrt(jnp.mean(x * x, axis=-1, keepdims=True) + RMS_EPS) * g_ref[...]
    o_ref[...] = jnp.dot(y.astype(jnp.bfloat16), w_ref[...], preferred_element_type=jnp.float32)


def norm_matmul(x, g, w):
    n, d = x.shape
    c = w.shape[1]
    return pl.pallas_call(
        _norm_matmul_kernel,
        grid=(n // ROW_TILE,),
        in_specs=[pl.BlockSpec((ROW_TILE, d), lambda i: (i, 0)),
                  pl.BlockSpec((1, d), lambda i: (0, 0)),
                  pl.BlockSpec((d, c), lambda i: (0, 0))],
        out_specs=pl.BlockSpec((ROW_TILE, c), lambda i: (i, 0)),
        out_shape=jax.ShapeDtypeStruct((n, c), jnp.float32),
        compiler_params=_compiler_params(),
        name="norm_matmul",
    )(x, g.reshape(1, d), w)


def _out_proj_kernel(h_ref, ya_ref, yb_ref, yc_ref, yd_ref, w_ref, o_ref):
    acc = h_ref[...]
    for gi, y_ref in enumerate((ya_ref, yb_ref, yc_ref, yd_ref)):
        acc += jnp.dot(y_ref[...].astype(jnp.bfloat16), w_ref[gi * GROUP_W:(gi + 1) * GROUP_W, :],
                       preferred_element_type=jnp.float32)
    o_ref[...] = acc


def out_proj(h, ys, w):
    n, d = h.shape
    y_spec = pl.BlockSpec((ROW_TILE, GROUP_W), lambda i: (i, 0))
    return pl.pallas_call(
        _out_proj_kernel,
        grid=(n // ROW_TILE,),
        in_specs=[pl.BlockSpec((ROW_TILE, d), lambda i: (i, 0)), y_spec, y_spec, y_spec, y_spec,
                  pl.BlockSpec(w.shape, lambda i: (0, 0))],
        out_specs=pl.BlockSpec((ROW_TILE, d), lambda i: (i, 0)),
        out_shape=jax.ShapeDtypeStruct((n, d), jnp.float32),
        compiler_params=_compiler_params(),
        name="out_proj",
    )(h, *ys, w)


def _rms_norm_kernel(x_ref, g_ref, o_ref):
    x = x_ref[...]
    o_ref[...] = x * lax.rsqrt(jnp.mean(x * x, axis=-1, keepdims=True) + RMS_EPS) * g_ref[...]


def rms_norm_rows(x, g):
    n, d = x.shape
    return pl.pallas_call(
        _rms_norm_kernel,
        grid=(n // ROW_TILE,),
        in_specs=[pl.BlockSpec((ROW_TILE, d), lambda i: (i, 0)), pl.BlockSpec((1, d), lambda i: (0, 0))],
        out_specs=pl.BlockSpec((ROW_TILE, d), lambda i: (i, 0)),
        out_shape=jax.ShapeDtypeStruct((n, d), jnp.float32),
        compiler_params=_compiler_params(),
        name="final_norm",
    )(x, g.reshape(1, d))


def _sortable_key(x):
    x = jnp.where(x == 0.0, 0.0, x)
    b = pltpu.bitcast(x, jnp.int32)
    return b ^ ((b >> 31) & 0x7FFFFFFF)


def _dsa_prompt_kernel(q_ref, qi_ref, w_ref, kidx_ref, k_ref, v_ref, o_ref,
                       key_ref, m_ref, l_ref, acc_ref, *, topk):
    i = pl.program_id(1)
    n_chunks = (i + 2) // 2
    q_pos = i * Q_BLOCK + lax.broadcasted_iota(jnp.int32, (Q_BLOCK, KEY_CHUNK), 0)
    lane = lax.broadcasted_iota(jnp.int32, (Q_BLOCK, KEY_CHUNK), 1)
    qi = qi_ref[...].reshape(N_IDX_HEADS * Q_BLOCK, IDX_DIM)
    w = w_ref[...]
    w_cols = [jnp.broadcast_to(w[:, h:h + 1], (Q_BLOCK, KEY_CHUNK)) for h in range(N_IDX_HEADS)]

    def score_chunk(c, carry):
        off = pl.multiple_of(c * KEY_CHUNK, KEY_CHUNK)
        kc = kidx_ref[pl.ds(off, KEY_CHUNK), :]
        s = lax.dot_general(qi, kc, (((1,), (1,)), ((), ())), preferred_element_type=jnp.float32)
        s = jnp.maximum(s, 0.0)
        score = w_cols[0] * s[0:Q_BLOCK]
        for h in range(1, N_IDX_HEADS):
            score = score + w_cols[h] * s[h * Q_BLOCK:(h + 1) * Q_BLOCK]
        visible = (off + lane) <= q_pos
        key_ref[c] = jnp.where(visible, _sortable_key(score), NEG_INF_KEY)
        return carry

    lax.fori_loop(0, n_chunks, score_chunk, 0)

    def count(pred):
        def body(c, acc):
            p = pred(key_ref[c]).astype(jnp.float32)
            return acc + p[:, :128] + p[:, 128:]
        acc = lax.fori_loop(0, n_chunks, body, jnp.zeros((Q_BLOCK, 128), jnp.float32))
        return jnp.sum(acc, axis=-1, keepdims=True)

    kf = float(topk)
    t0 = jnp.where(count(lambda k: k >= 0) >= kf, 0, INT_MIN).astype(jnp.int32)

    def search(it, t):
        cand = t + (jnp.int32(1) << (30 - it))
        return jnp.where(count(lambda k: k >= cand) >= kf, cand, t)

    thr = lax.fori_loop(0, 31, search, t0)
    n_take_eq = kf - count(lambda k: k > thr)

    head_of_lane = lax.broadcasted_iota(jnp.int32, (Q_BLOCK, GROUP_W), 1) // HEAD_DIM
    q = q_ref[...]
    q_heads = [jnp.where(head_of_lane == h, q, jnp.zeros_like(q)) for h in range(N_HEADS)]
    tri = (lax.broadcasted_iota(jnp.int32, (KEY_CHUNK, KEY_CHUNK), 0)
           < lax.broadcasted_iota(jnp.int32, (KEY_CHUNK, KEY_CHUNK), 1)).astype(jnp.bfloat16)

    m_ref[...] = jnp.full(m_ref.shape, -jnp.inf, jnp.float32)
    l_ref[...] = jnp.zeros(l_ref.shape, jnp.float32)
    acc_ref[...] = jnp.zeros(acc_ref.shape, jnp.float32)

    def attend_chunk(c, eq_before):
        off = pl.multiple_of(c * KEY_CHUNK, KEY_CHUNK)
        keys = key_ref[c]
        eq = keys == thr
        eq_excl = jnp.dot(eq.astype(jnp.bfloat16), tri, preferred_element_type=jnp.float32)
        sel = (keys > thr) | (eq & ((eq_before + eq_excl) < n_take_eq))
        sel = sel & (keys != NEG_INF_KEY)
        kc = k_ref[pl.ds(off, KEY_CHUNK), :]
        vc = v_ref[pl.ds(off, KEY_CHUNK), :]
        acc = acc_ref[...]
        for h in range(N_HEADS):
            lg = lax.dot_general(q_heads[h], kc, (((1,), (1,)), ((), ())),
                                 preferred_element_type=jnp.float32) * (HEAD_DIM ** -0.5)
            lg = jnp.where(sel, lg, -jnp.inf)
            m_old = m_ref[h]
            m_new = jnp.maximum(m_old, jnp.max(lg, axis=-1, keepdims=True))
            m_safe = jnp.where(m_new == -jnp.inf, 0.0, m_new)
            p = jnp.exp(lg - m_safe)
            alpha = jnp.exp(m_old - m_safe)
            l_ref[h] = alpha * l_ref[h] + jnp.sum(p, axis=-1, keepdims=True)
            m_ref[h] = m_new
            pv = jnp.dot(p.astype(jnp.bfloat16), vc, preferred_element_type=jnp.float32)
            acc = jnp.where(head_of_lane == h, alpha * acc + pv, acc)
        acc_ref[...] = acc
        return eq_before + jnp.sum(eq.astype(jnp.float32), axis=-1, keepdims=True)

    lax.fori_loop(0, n_chunks, attend_chunk, jnp.zeros((Q_BLOCK, 1), jnp.float32))

    acc = acc_ref[...]
    out = jnp.zeros_like(acc)
    for h in range(N_HEADS):
        out = jnp.where(head_of_lane == h, acc / l_ref[h], out)
    o_ref[...] = out


def dsa_prompt(q, k, v, q_idx, k_idx, w_idx, topk):
    b, t, _ = q.shape
    n_blk = pl.cdiv(t, Q_BLOCK)
    tq = n_blk * Q_BLOCK
    tk = pl.cdiv(tq, KEY_CHUNK) * KEY_CHUNK
    bf = jnp.bfloat16
    pad_q = lambda a: jnp.pad(a, [(0, 0), (0, tq - t), (0, 0)])
    pad_k = lambda a: jnp.pad(a, [(0, 0), (0, tk - t), (0, 0)])
    qi = pad_q(q_idx).astype(bf).reshape(b, tq, N_IDX_HEADS, IDX_DIM).transpose(0, 2, 1, 3)
    out = pl.pallas_call(
        functools.partial(_dsa_prompt_kernel, topk=topk),
        grid=(b, n_blk),
        in_specs=[
            pl.BlockSpec((None, Q_BLOCK, GROUP_W), lambda bi, i: (bi, i, 0)),
            pl.BlockSpec((None, N_IDX_HEADS, Q_BLOCK, IDX_DIM), lambda bi, i: (bi, 0, i, 0)),
            pl.BlockSpec((None, Q_BLOCK, N_IDX_HEADS), lambda bi, i: (bi, i, 0)),
            pl.BlockSpec((None, tk, IDX_DIM), lambda bi, i: (bi, 0, 0)),
            pl.BlockSpec((None, tk, GROUP_W), lambda bi, i: (bi, 0, 0)),
            pl.BlockSpec((None, tk, GROUP_W), lambda bi, i: (bi, 0, 0)),
        ],
        out_specs=pl.BlockSpec((None, Q_BLOCK, GROUP_W), lambda bi, i: (bi, i, 0)),
        out_shape=jax.ShapeDtypeStruct((b, tq, GROUP_W), jnp.float32),
        scratch_shapes=[
            pltpu.VMEM((tk // KEY_CHUNK, Q_BLOCK, KEY_CHUNK), jnp.int32),
            pltpu.VMEM((N_HEADS, Q_BLOCK, 1), jnp.float32),
            pltpu.VMEM((N_HEADS, Q_BLOCK, 1), jnp.float32),
            pltpu.VMEM((Q_BLOCK, GROUP_W), jnp.float32),
        ],
        compiler_params=_compiler_params(),
        name="dsa_prompt",
    )(pad_q(q).astype(bf), qi, pad_q(w_idx), pad_k(k_idx).astype(bf), pad_k(k).astype(bf), pad_k(v).astype(bf))
    return out[:, :t]


def indexer_topk(q_idx, w_idx, k_idx, q_pos, topk):
    s = jnp.einsum('bqhd,bld->bqhl', q_idx, k_idx, preferred_element_type=jnp.float32)
    score = jnp.einsum('bqh,bqhl->bql', w_idx, jax.nn.relu(s))
    visible = jnp.arange(k_idx.shape[1])[None, :] <= q_pos[:, None]
    score = jnp.where(visible[None], score, -jnp.inf)
    top_score, top_idx = lax.top_k(score, topk)
    return top_idx, top_score > -jnp.inf


def attend_selected(q, k_sel, v_sel, valid):
    logits = jnp.einsum('bqhd,bqkhd->bqhk', q, k_sel, preferred_element_type=jnp.float32) * (HEAD_DIM ** -0.5)
    logits = jnp.where(valid[:, :, None, :], logits, -jnp.inf)
    p = jax.nn.softmax(logits, axis=-1)
    return jnp.einsum('bqhk,bqkhd->bqhd', p, v_sel)


def take_rows(a, idx):
    return jax.vmap(lambda arr, i: arr[i])(a, idx)


def dsa_sample(q, k, v, q_idx, k_idx, w_idx, k_pool, v_pool, kidx_pool, page_table):
    db, t = q.shape[:2]
    n_pages = page_table.shape[1]
    past_len = n_pages * PAGE_SIZE
    topk = min(TOPK_MAX, (past_len + t) // 4)
    kidx_past = kidx_pool[page_table].reshape(db, past_len, IDX_DIM)
    kidx_all = jnp.concatenate([kidx_past, k_idx], axis=1)
    q_pos = past_len + jnp.arange(t)
    top_idx, valid = indexer_topk(q_idx, w_idx, kidx_all, q_pos, topk)
    is_past = (top_idx < past_len)[..., None, None]
    phys = take_rows(page_table, jnp.minimum(top_idx // PAGE_SIZE, n_pages - 1))
    off = top_idx % PAGE_SIZE
    new_i = jnp.clip(top_idx - past_len, 0, t - 1)
    k_sel = jnp.where(is_past, k_pool[phys, off], take_rows(k, new_i))
    v_sel = jnp.where(is_past, v_pool[phys, off], take_rows(v, new_i))
    return attend_selected(q, k_sel, v_sel, valid)


def layer_norm(x, g, b):
    mu = jnp.mean(x, axis=-1, keepdims=True)
    var = jnp.mean(jnp.square(x - mu), axis=-1, keepdims=True)
    return (x - mu) * lax.rsqrt(var + LN_EPS) * g + b


def causal_dwconv(x_ext, w):
    c = x_ext.shape[-1]
    return lax.conv_general_dilated(x_ext, w[:, None, :], window_strides=(1,), padding='VALID',
                                    dimension_numbers=('NWC', 'WIO', 'NWC'), feature_group_count=c)


def multiscale_pool(x_ext, n_ctx, start_pos):
    b, l, _ = x_ext.shape
    t = l - n_ctx
    cs = jnp.concatenate([jnp.zeros((b, 1, GROUP_W), jnp.float32), jnp.cumsum(x_ext, axis=1)], axis=1)
    rows = jnp.arange(n_ctx, l)
    pos = start_pos + jnp.arange(t)
    outs = []
    for gi, w in enumerate(POOL_WINDOWS):
        csg = cs[..., gi * POOL_GROUP:(gi + 1) * POOL_GROUP]
        lo = jnp.maximum(rows + 1 - w, 0)
        cnt = jnp.minimum(pos + 1, w).astype(jnp.float32)
        outs.append((csg[:, rows + 1] - csg[:, lo]) / cnt[None, :, None])
    mean = jnp.concatenate(outs, axis=-1)
    return mean - x_ext[:, n_ctx:]


def mixer_block(z, lw, past, start_pos):
    b, t, _ = z.shape
    parts, start = [], 0
    for size in SPLIT_SIZES:
        parts.append(z[..., start:start + size])
        start += size
    (a_val, a_gate, b_gate, c_gate, b_in, q, k, v, q_idx, k_idx, w_idx, pool_in) = parts
    u = a_val * jax.nn.sigmoid(a_gate)
    ctx_a = jnp.zeros((b, CONV_A_WIDTH - 1, GROUP_W), u.dtype) if past is None else past['conv_a']
    u_ext = jnp.concatenate([ctx_a, u], axis=1)
    y_a = causal_dwconv(u_ext, lw['conv_a_w']) + lw['conv_a_b']
    y_a = jax.nn.silu(layer_norm(y_a, lw['ln_a_g'], lw['ln_a_b']))
    g_in = c_gate * b_in
    ctx_b = jnp.zeros((b, CONV_B_WIDTH - 1, GROUP_W), g_in.dtype) if past is None else past['conv_b']
    g_ext = jnp.concatenate([ctx_b, g_in], axis=1)
    y_b = b_gate * causal_dwconv(g_ext, lw['conv_b_w'])
    if past is None:
        y_c = dsa_prompt(q, k, v, q_idx, k_idx, w_idx, min(TOPK_MAX, t // 4))
    else:
        y_c = dsa_sample(q.reshape(b, t, N_HEADS, HEAD_DIM), k.reshape(b, t, N_HEADS, HEAD_DIM),
                         v.reshape(b, t, N_HEADS, HEAD_DIM), q_idx.reshape(b, t, N_IDX_HEADS, IDX_DIM), k_idx, w_idx,
                         past['k_pool'], past['v_pool'], past['kidx_pool'], past['page_table']).reshape(b, t, GROUP_W)
    if past is None:
        p_ext, n_ctx = pool_in, 0
    else:
        p_ext, n_ctx = jnp.concatenate([past['pool'], pool_in], axis=1), POOL_CTX
    pooled = multiscale_pool(p_ext, n_ctx, start_pos).reshape(b, t, len(POOL_WINDOWS), POOL_GROUP)
    y_d = jnp.einsum('btgc,gcd->btgd', pooled, lw['pool_w']).reshape(b, t, GROUP_W) * lw['pool_scale']
    new = (k.reshape(b, t, N_HEADS, HEAD_DIM), v.reshape(b, t, N_HEADS, HEAD_DIM), k_idx,
           u_ext[:, -(CONV_A_WIDTH - 1):], g_ext[:, -(CONV_B_WIDTH - 1):], p_ext[:, -POOL_CTX:])
    return (y_a, y_b, y_c, y_d), new


def _top_rows(s, k):
    r = s.shape[0]
    row = lax.broadcasted_iota(jnp.int32, s.shape, 0)
    vals, idxs = [], []
    for _ in range(k):
        m = jnp.max(s, axis=0, keepdims=True)
        idx = jnp.min(jnp.where(s == m, row, r), axis=0, keepdims=True)
        s = jnp.where(row == idx, -jnp.inf, s)
        vals.append(m)
        idxs.append(idx)
    return jnp.concatenate(vals, axis=0), jnp.concatenate(idxs, axis=0)


def _pick_rows(table, which):
    row = lax.broadcasted_iota(jnp.int32, table.shape, 0)
    out = []
    for j in range(which.shape[0]):
        out.append(jnp.sum(jnp.where(row == which[j:j + 1], table, 0), axis=0, keepdims=True))
    return jnp.concatenate(out, axis=0)


def _peer_route_kernel(h_ref, g_ref, wqt_ref, sk_ref, xn_ref, k1_ref, k2_ref, gate_ref,
                       qt_ref, k1t_ref, k2t_ref, gt_ref):
    x = h_ref[...]
    xn = (x * lax.rsqrt(jnp.mean(x * x, axis=-1, keepdims=True) + RMS_EPS) * g_ref[...]).astype(jnp.bfloat16)
    xn_ref[...] = xn
    qt_ref[...] = lax.dot_general(wqt_ref[...], xn, (((1,), (1,)), ((), ())),
                                  preferred_element_type=jnp.float32).astype(jnp.bfloat16)

    def head(h, carry):
        r0 = pl.multiple_of(h * 2 * PEER_HALF, 2 * PEER_HALF)
        s1 = jnp.dot(sk_ref[2 * h], qt_ref[pl.ds(r0, PEER_HALF), :], preferred_element_type=jnp.float32)
        s2 = jnp.dot(sk_ref[2 * h + 1], qt_ref[pl.ds(r0 + PEER_HALF, PEER_HALF), :],
                     preferred_element_type=jnp.float32)
        v1, i1 = _top_rows(s1, PEER_TOPK)
        v2, i2 = _top_rows(s2, PEER_TOPK)
        cand = jnp.concatenate([v1[a:a + 1] + v2 for a in range(PEER_TOPK)], axis=0)
        best, flat = _top_rows(cand, PEER_TOPK)
        e = jnp.exp(best - best[0:1])
        o0 = pl.multiple_of(h * PEER_TOPK, PEER_TOPK)
        k1t_ref[pl.ds(o0, PEER_TOPK), :] = _pick_rows(i1, flat >> 4)
        k2t_ref[pl.ds(o0, PEER_TOPK), :] = _pick_rows(i2, flat & (PEER_TOPK - 1))
        gt_ref[pl.ds(o0, PEER_TOPK), :] = e / jnp.sum(e, axis=0, keepdims=True)
        return carry

    lax.fori_loop(0, PEER_HEADS, head, 0)
    k1_ref[...] = k1t_ref[...].T
    k2_ref[...] = k2t_ref[...].T
    gate_ref[...] = gt_ref[...].T


def peer_route(h, g, wqt, sk):
    n, d = h.shape
    tm = ROUTE_TILE
    row_spec = pl.BlockSpec((tm, N_ROUTES), lambda i: (i, 0))
    return pl.pallas_call(
        _peer_route_kernel,
        grid=(n // tm,),
        in_specs=[pl.BlockSpec((tm, d), lambda i: (i, 0)),
                  pl.BlockSpec((1, d), lambda i: (0, 0)),
                  pl.BlockSpec(wqt.shape, lambda i: (0, 0)),
                  pl.BlockSpec(sk.shape, lambda i: (0, 0, 0))],
        out_specs=[pl.BlockSpec((tm, d), lambda i: (i, 0)), row_spec, row_spec, row_spec],
        out_shape=[jax.ShapeDtypeStruct((n, d), jnp.bfloat16),
                   jax.ShapeDtypeStruct((n, N_ROUTES), jnp.int32),
                   jax.ShapeDtypeStruct((n, N_ROUTES), jnp.int32),
                   jax.ShapeDtypeStruct((n, N_ROUTES), jnp.float32)],
        scratch_shapes=[pltpu.VMEM((PEER_HEADS * 2 * PEER_HALF, tm), jnp.bfloat16),
                        pltpu.VMEM((N_ROUTES, tm), jnp.int32),
                        pltpu.VMEM((N_ROUTES, tm), jnp.int32),
                        pltpu.VMEM((N_ROUTES, tm), jnp.float32)],
        compiler_params=_compiler_params(),
        name="peer_route",
    )(h, g.reshape(1, d), wqt, sk)


def _peer_eval_kernel(xn_ref, k1_ref, k2_ref, gate_ref, u_ref, v_ref, h_ref, o_ref, w_ref):
    j = pl.program_id(1)
    tn = xn_ref.shape[0]

    @pl.when(j == 0)
    def _():
        sub = lax.broadcasted_iota(jnp.int32, (N_KEYS, N_ROUTES), 0)

        def token(n, carry):
            k1 = jnp.broadcast_to(k1_ref[pl.ds(n, 1), :], (N_KEYS, N_ROUTES))
            k2 = jnp.broadcast_to(k2_ref[pl.ds(n, 1), :], (N_KEYS, N_ROUTES))
            gt = jnp.broadcast_to(gate_ref[pl.ds(n, 1), :], (N_KEYS, N_ROUTES))
            left = jnp.where(sub == k1, gt, 0.0)
            left_hi = left.astype(jnp.bfloat16)
            left_lo = (left - left_hi.astype(jnp.float32)).astype(jnp.bfloat16)
            right = (sub == k2).astype(jnp.bfloat16)
            tile = lax.dot_general(jnp.concatenate([left_hi, left_lo], axis=1),
                                   jnp.concatenate([right, right], axis=1),
                                   (((1,), (1,)), ((), ())), preferred_element_type=jnp.float32)
            w_ref[pl.ds(pl.multiple_of(n * N_KEYS, N_KEYS), N_KEYS), :] = tile
            return carry

        lax.fori_loop(0, tn, token, 0)
        o_ref[...] = h_ref[...]

    act = lax.dot_general(xn_ref[...], u_ref[...], (((1,), (1,)), ((), ())), preferred_element_type=jnp.float32)
    wgt = jnp.concatenate([w_ref[pl.ds(j * KEYS_PER_STEP + c, tn, stride=N_KEYS), :]
                           for c in range(KEYS_PER_STEP)], axis=1)
    coef = (wgt * jax.nn.gelu(act)).astype(jnp.bfloat16)
    o_ref[...] += jnp.dot(coef, v_ref[...], preferred_element_type=jnp.float32)


def peer_eval(xn, k1, k2, gate, u, v, h):
    n, d = xn.shape
    tn = PEER_TILE
    eb = KEYS_PER_STEP * N_KEYS
    tok = lambda i, j: (i, 0)
    return pl.pallas_call(
        _peer_eval_kernel,
        grid=(n // tn, u.shape[0] // eb),
        in_specs=[pl.BlockSpec((tn, d), tok),
                  pl.BlockSpec((tn, N_ROUTES), tok), pl.BlockSpec((tn, N_ROUTES), tok),
                  pl.BlockSpec((tn, N_ROUTES), tok),
                  pl.BlockSpec((eb, d), lambda i, j: (j, 0)),
                  pl.BlockSpec((eb, d), lambda i, j: (j, 0)),
                  pl.BlockSpec((tn, d), tok)],
        out_specs=pl.BlockSpec((tn, d), tok),
        out_shape=jax.ShapeDtypeStruct((n, d), jnp.float32),
        scratch_shapes=[pltpu.VMEM((tn * N_KEYS, N_KEYS), jnp.float32)],
        compiler_params=_compiler_params(dimension_semantics=("arbitrary", "arbitrary")),
        name="peer_eval",
    )(xn, k1, k2, gate, u, v, h)


def kernel(x_prompt, x_sample, cache_k, cache_v, cache_kidx, state_conv_a, state_conv_b, state_pool,
           page_table, meta_tokens, w_in, conv_a_w, conv_a_b, ln_a_g, ln_a_b, conv_b_w, pool_w,
           pool_scale, w_out, norm_mix_g, norm_ffn_g, peer_wq, peer_subkeys, peer_u, peer_v, norm_final_g):
    b, seq, d = x_prompt.shape
    db, dt, _ = x_sample.shape
    depth = w_in.shape[0]
    tp = N_META + seq
    n_p, n_s = b * tp, db * dt
    n_tok = n_p + n_s
    n_pad = pl.cdiv(n_tok, PEER_TILE) * PEER_TILE
    bf = jnp.bfloat16
    past_len = page_table.shape[1] * PAGE_SIZE

    meta = jnp.broadcast_to(meta_tokens[None], (b, N_META, d))
    hp = jnp.concatenate([meta, x_prompt], axis=1).reshape(n_p, d)
    h = jnp.concatenate([hp, x_sample.reshape(n_s, d), jnp.zeros((n_pad - n_tok, d), jnp.float32)], axis=0)

    news_p, news_s = [], []
    for l in range(depth):
        lw = {'conv_a_w': conv_a_w[l], 'conv_a_b': conv_a_b[l], 'ln_a_g': ln_a_g[l], 'ln_a_b': ln_a_b[l],
              'conv_b_w': conv_b_w[l], 'pool_w': pool_w[l], 'pool_scale': pool_scale[l]}
        past = {'k_pool': cache_k[l], 'v_pool': cache_v[l], 'kidx_pool': cache_kidx[l],
                'conv_a': state_conv_a[l], 'conv_b': state_conv_b[l], 'pool': state_pool[l],
                'page_table': page_table}
        z = norm_matmul(h, norm_mix_g[l], w_in[l].astype(bf))
        ys_p, new_p = mixer_block(z[:n_p].reshape(b, tp, N_COLS), lw, None, 0)
        ys_s, new_s = mixer_block(z[n_p:n_tok].reshape(db, dt, N_COLS), lw, past, past_len)
        ys = [jnp.concatenate([yp.reshape(n_p, GROUP_W), ysm.reshape(n_s, GROUP_W),
                               jnp.zeros((n_pad - n_tok, GROUP_W), jnp.float32)], axis=0)
              for yp, ysm in zip(ys_p, ys_s)]
        h = out_proj(h, ys, w_out[l].astype(bf))
        xn, k1, k2, gate = peer_route(h, norm_ffn_g[l], peer_wq[l].T.astype(bf),
                                      peer_subkeys[l].reshape(PEER_HEADS * 2, N_KEYS, PEER_HALF).astype(bf))
        h = peer_eval(xn, k1, k2, gate, peer_u[l].astype(bf), peer_v[l].astype(bf), h)
        news_p.append(new_p)
        news_s.append(new_s)

    y = rms_norm_rows(h, norm_final_g)
    y_prompt = y[:n_p].reshape(b, tp, d)[:, N_META:]
    y_sample = y[n_p:n_tok].reshape(db, dt, d)
    outs_p = [jnp.stack([n[i] for n in news_p]) for i in range(6)]
    outs_s = [jnp.stack([n[i] for n in news_s]) for i in range(6)]
    return (y_prompt, y_sample, *outs_p, *outs_s)
```

```python
import functools

import jax
import jax.numpy as jnp
from jax import lax
from jax.experimental import pallas as pl
from jax.experimental.pallas import tpu as pltpu

D_MODEL = 1024
N_META = 16
GROUP_W = 256
CONV_A_WIDTH = 31
CONV_B_WIDTH = 3
N_HEADS = 4
HEAD_DIM = 64
N_IDX_HEADS = 8
IDX_DIM = 64
TOPK_MAX = 256
PAGE_SIZE = 128
POOL_WINDOWS = (2, 4, 8, 16)
POOL_GROUP = 64
POOL_CTX = 15
PEER_HEADS = 8
N_KEYS = 128
PEER_HALF = 128
PEER_TOPK = 16
N_ROUTES = PEER_HEADS * PEER_TOPK
RMS_EPS = 1e-6
LN_EPS = 1e-5
SPLIT_SIZES = (GROUP_W,) * 8 + (N_IDX_HEADS * IDX_DIM, IDX_DIM, N_IDX_HEADS, GROUP_W)
N_COLS = sum(SPLIT_SIZES)

Q_BLOCK = 128
KEY_CHUNK = 256
ROW_TILE = 256
ROUTE_TILE = 128
PEER_TILE = 256
KEYS_PER_STEP = 8
GATE_BUILD_UNROLL = 16
SAMPLE_ROWS = 8
PAGES_PER_CHUNK = 4
SAMPLE_CHUNK = PAGES_PER_CHUNK * PAGE_SIZE
INT_MIN = -2 ** 31
NEG_INF_KEY = -2139095041
VMEM_LIMIT_BYTES = 48 * 1024 * 1024


def _compiler_params(**kw):
    return pltpu.CompilerParams(vmem_limit_bytes=VMEM_LIMIT_BYTES, **kw)


def _norm_matmul_kernel(x_ref, g_ref, w_ref, o_ref):
    x = x_ref[...]
    y = x * lax.rsqrt(jnp.mean(x * x, axis=-1, keepdims=True) + RMS_EPS) * g_ref[...]
    o_ref[...] = jnp.dot(y.astype(jnp.bfloat16), w_ref[...], preferred_element_type=jnp.float32)


def norm_matmul(x, g, w):
    n, d = x.shape
    c = w.shape[1]
    return pl.pallas_call(
        _norm_matmul_kernel,
        grid=(n // ROW_TILE,),
        in_specs=[pl.BlockSpec((ROW_TILE, d), lambda i: (i, 0)),
                  pl.BlockSpec((1, d), lambda i: (0, 0)),
                  pl.BlockSpec((d, c), lambda i: (0, 0))],
        out_specs=pl.BlockSpec((ROW_TILE, c), lambda i: (i, 0)),
        out_shape=jax.ShapeDtypeStruct((n, c), jnp.float32),
        compiler_params=_compiler_params(),
        name="norm_matmul",
    )(x, g.reshape(1, d), w)


def _out_proj_kernel(h_ref, ya_ref, yb_ref, yc_ref, yd_ref, w_ref, o_ref):
    acc = h_ref[...]
    for gi, y_ref in enumerate((ya_ref, yb_ref, yc_ref, yd_ref)):
        acc += jnp.dot(y_ref[...].astype(jnp.bfloat16), w_ref[gi * GROUP_W:(gi + 1) * GROUP_W, :],
                       preferred_element_type=jnp.float32)
    o_ref[...] = acc


def out_proj(h, ys, w):
    n, d = h.shape
    y_spec = pl.BlockSpec((ROW_TILE, GROUP_W), lambda i: (i, 0))
    return pl.pallas_call(
        _out_proj_kernel,
        grid=(n // ROW_TILE,),
        in_specs=[pl.BlockSpec((ROW_TILE, d), lambda i: (i, 0)), y_spec, y_spec, y_spec, y_spec,
                  pl.BlockSpec(w.shape, lambda i: (0, 0))],
        out_specs=pl.BlockSpec((ROW_TILE, d), lambda i: (i, 0)),
        out_shape=jax.ShapeDtypeStruct((n, d), jnp.float32),
        compiler_params=_compiler_params(),
        name="out_proj",
    )(h, *ys, w)


def _rms_norm_kernel(x_ref, g_ref, o_ref):
    x = x_ref[...]
    o_ref[...] = x * lax.rsqrt(jnp.mean(x * x, axis=-1, keepdims=True) + RMS_EPS) * g_ref[...]


def rms_norm_rows(x, g):
    n, d = x.shape
    return pl.pallas_call(
        _rms_norm_kernel,
        grid=(n // ROW_TILE,),
        in_specs=[pl.BlockSpec((ROW_TILE, d), lambda i: (i, 0)), pl.BlockSpec((1, d), lambda i: (0, 0))],
        out_specs=pl.BlockSpec((ROW_TILE, d), lambda i: (i, 0)),
        out_shape=jax.ShapeDtypeStruct((n, d), jnp.float32),
        compiler_params=_compiler_params(),
        name="final_norm",
    )(x, g.reshape(1, d))


def _sortable_key(x):
    x = jnp.where(x == 0.0, 0.0, x)
    b = pltpu.bitcast(x, jnp.int32)
    return b ^ ((b >> 31) & 0x7FFFFFFF)


def _dsa_prompt_kernel(q_ref, qi_ref, w_ref, kidx_ref, k_ref, v_ref, o_ref,
                       key_ref, m_ref, l_ref, acc_ref, *, topk):
    i = pl.program_id(1)
    n_chunks = (i + 2) // 2
    q_pos = i * Q_BLOCK + lax.broadcasted_iota(jnp.int32, (Q_BLOCK, KEY_CHUNK), 0)
    lane = lax.broadcasted_iota(jnp.int32, (Q_BLOCK, KEY_CHUNK), 1)
    qi = qi_ref[...].reshape(N_IDX_HEADS * Q_BLOCK, IDX_DIM)
    w = w_ref[...]
    w_cols = [jnp.broadcast_to(w[:, h:h + 1], (Q_BLOCK, KEY_CHUNK)) for h in range(N_IDX_HEADS)]

    def score_chunk(c, carry):
        off = pl.multiple_of(c * KEY_CHUNK, KEY_CHUNK)
        kc = kidx_ref[pl.ds(off, KEY_CHUNK), :]
        s = lax.dot_general(qi, kc, (((1,), (1,)), ((), ())), preferred_element_type=jnp.float32)
        s = jnp.maximum(s, 0.0)
        score = w_cols[0] * s[0:Q_BLOCK]
        for h in range(1, N_IDX_HEADS):
            score = score + w_cols[h] * s[h * Q_BLOCK:(h + 1) * Q_BLOCK]
        visible = (off + lane) <= q_pos
        key_ref[c] = jnp.where(visible, _sortable_key(score), NEG_INF_KEY)
        return carry

    lax.fori_loop(0, n_chunks, score_chunk, 0)

    def count(pred):
        def body(c, acc):
            p = pred(key_ref[c]).astype(jnp.float32)
            return acc + p[:, :128] + p[:, 128:]
        acc = lax.fori_loop(0, n_chunks, body, jnp.zeros((Q_BLOCK, 128), jnp.float32))
        return jnp.sum(acc, axis=-1, keepdims=True)

    kf = float(topk)
    t0 = jnp.where(count(lambda k: k >= 0) >= kf, 0, INT_MIN).astype(jnp.int32)

    def search(it, t):
        cand = t + (jnp.int32(1) << (30 - it))
        return jnp.where(count(lambda k: k >= cand) >= kf, cand, t)

    thr = lax.fori_loop(0, 31, search, t0)
    n_take_eq = kf - count(lambda k: k > thr)

    head_of_lane = lax.broadcasted_iota(jnp.int32, (Q_BLOCK, GROUP_W), 1) // HEAD_DIM
    q = q_ref[...]
    q_heads = jnp.concatenate([jnp.where(head_of_lane == h, q, jnp.zeros_like(q)) for h in range(N_HEADS)], axis=0)
    tri = (lax.broadcasted_iota(jnp.int32, (KEY_CHUNK, KEY_CHUNK), 0)
           < lax.broadcasted_iota(jnp.int32, (KEY_CHUNK, KEY_CHUNK), 1)).astype(jnp.bfloat16)

    m_ref[...] = jnp.full(m_ref.shape, -jnp.inf, jnp.float32)
    l_ref[...] = jnp.zeros(l_ref.shape, jnp.float32)
    acc_ref[...] = jnp.zeros(acc_ref.shape, jnp.float32)

    def attend_chunk(c, eq_before):
        off = pl.multiple_of(c * KEY_CHUNK, KEY_CHUNK)
        keys = key_ref[c]
        eq = keys == thr
        eq_excl = jnp.dot(eq.astype(jnp.bfloat16), tri, preferred_element_type=jnp.float32)
        sel = (keys > thr) | (eq & ((eq_before + eq_excl) < n_take_eq))
        sel = sel & (keys != NEG_INF_KEY)
        sel_all = jnp.concatenate([sel] * N_HEADS, axis=0)
        kc = k_ref[pl.ds(off, KEY_CHUNK), :]
        vc = v_ref[pl.ds(off, KEY_CHUNK), :]
        lg = lax.dot_general(q_heads, kc, (((1,), (1,)), ((), ())),
                             preferred_element_type=jnp.float32) * (HEAD_DIM ** -0.5)
        lg = jnp.where(sel_all, lg, -jnp.inf)
        m_old = m_ref[...]
        m_new = jnp.maximum(m_old, jnp.max(lg, axis=-1, keepdims=True))
        m_safe = jnp.where(m_new == -jnp.inf, 0.0, m_new)
        p = jnp.exp(lg - m_safe)
        alpha = jnp.exp(m_old - m_safe)
        l_ref[...] = alpha * l_ref[...] + jnp.sum(p, axis=-1, keepdims=True)
        m_ref[...] = m_new
        acc_ref[...] = alpha * acc_ref[...] + jnp.dot(p.astype(jnp.bfloat16), vc, preferred_element_type=jnp.float32)
        return eq_before + jnp.sum(eq.astype(jnp.float32), axis=-1, keepdims=True)

    lax.fori_loop(0, n_chunks, attend_chunk, jnp.zeros((Q_BLOCK, 1), jnp.float32))

    res = acc_ref[...] / l_ref[...]
    out = jnp.zeros((Q_BLOCK, GROUP_W), jnp.float32)
    for h in range(N_HEADS):
        out = jnp.where(head_of_lane == h, res[h * Q_BLOCK:(h + 1) * Q_BLOCK], out)
    o_ref[...] = out


def dsa_prompt(q, k, v, q_idx, k_idx, w_idx, topk):
    b, t, _ = q.shape
    n_blk = pl.cdiv(t, Q_BLOCK)
    tq = n_blk * Q_BLOCK
    tk = pl.cdiv(tq, KEY_CHUNK) * KEY_CHUNK
    bf = jnp.bfloat16
    pad_q = lambda a: jnp.pad(a, [(0, 0), (0, tq - t), (0, 0)])
    pad_k = lambda a: jnp.pad(a, [(0, 0), (0, tk - t), (0, 0)])
    qi = pad_q(q_idx).astype(bf).reshape(b, tq, N_IDX_HEADS, IDX_DIM).transpose(0, 2, 1, 3)
    out = pl.pallas_call(
        functools.partial(_dsa_prompt_kernel, topk=topk),
        grid=(b, n_blk),
        in_specs=[
            pl.BlockSpec((None, Q_BLOCK, GROUP_W), lambda bi, i: (bi, i, 0)),
            pl.BlockSpec((None, N_IDX_HEADS, Q_BLOCK, IDX_DIM), lambda bi, i: (bi, 0, i, 0)),
            pl.BlockSpec((None, Q_BLOCK, N_IDX_HEADS), lambda bi, i: (bi, i, 0)),
            pl.BlockSpec((None, tk, IDX_DIM), lambda bi, i: (bi, 0, 0)),
            pl.BlockSpec((None, tk, GROUP_W), lambda bi, i: (bi, 0, 0)),
            pl.BlockSpec((None, tk, GROUP_W), lambda bi, i: (bi, 0, 0)),
        ],
        out_specs=pl.BlockSpec((None, Q_BLOCK, GROUP_W), lambda bi, i: (bi, i, 0)),
        out_shape=jax.ShapeDtypeStruct((b, tq, GROUP_W), jnp.float32),
        scratch_shapes=[
            pltpu.VMEM((tk // KEY_CHUNK, Q_BLOCK, KEY_CHUNK), jnp.int32),
            pltpu.VMEM((N_HEADS * Q_BLOCK, 1), jnp.float32),
            pltpu.VMEM((N_HEADS * Q_BLOCK, 1), jnp.float32),
            pltpu.VMEM((N_HEADS * Q_BLOCK, GROUP_W), jnp.float32),
        ],
        compiler_params=_compiler_params(),
        name="dsa_prompt",
    )(pad_q(q).astype(bf), qi, pad_q(w_idx), pad_k(k_idx).astype(bf), pad_k(k).astype(bf), pad_k(v).astype(bf))
    return out[:, :t]


def _dsa_sample_kernel(pt_ref, q_ref, qi_ref, w_ref, kin_ref, kn_ref, vn_ref, kidx_hbm, k_hbm, v_hbm, o_ref,
                       kidx_buf, k_buf, v_buf, key_ref, m_ref, l_ref, acc_ref, sems, *, layer, n_pages, topk):
    b = pl.program_id(0)
    n_chunks = key_ref.shape[0]
    n_buf_pages = kidx_buf.shape[0]
    rows = SAMPLE_ROWS

    def page_copy(p, which):
        src, dst = ((kidx_hbm, kidx_buf), (k_hbm, k_buf), (v_hbm, v_buf))[which]
        return pltpu.make_async_copy(src.at[layer, pt_ref[b, p]], dst.at[p], sems.at[which])

    def start_page(p, carry):
        for which in range(3):
            page_copy(p, which).start()
        return carry

    lax.fori_loop(0, n_pages, start_page, 0)

    for p in range(n_pages, n_buf_pages):
        kidx_buf[p] = kin_ref[...] if p == n_pages else jnp.zeros(kidx_buf.shape[1:], jnp.float32)
        k_buf[p] = kn_ref[...] if p == n_pages else jnp.zeros(k_buf.shape[1:], jnp.float32)
        v_buf[p] = vn_ref[...] if p == n_pages else jnp.zeros(v_buf.shape[1:], jnp.float32)

    def wait_pages(which):
        def body(p, carry):
            page_copy(p, which).wait()
            return carry
        lax.fori_loop(0, n_pages, body, 0)

    wait_pages(0)

    past_len = n_pages * PAGE_SIZE
    q_row = lax.broadcasted_iota(jnp.int32, (rows, SAMPLE_CHUNK), 0)
    lane = lax.broadcasted_iota(jnp.int32, (rows, SAMPLE_CHUNK), 1)
    qi = qi_ref[...].reshape(N_IDX_HEADS * rows, IDX_DIM)
    w = w_ref[...]
    w_cols = [jnp.broadcast_to(w[:, h:h + 1], (rows, SAMPLE_CHUNK)) for h in range(N_IDX_HEADS)]

    def score_chunk(c, carry):
        p0 = pl.multiple_of(c * PAGES_PER_CHUNK, PAGES_PER_CHUNK)
        kc = kidx_buf[pl.ds(p0, PAGES_PER_CHUNK)].reshape(SAMPLE_CHUNK, IDX_DIM).astype(jnp.bfloat16)
        s = lax.dot_general(qi, kc, (((1,), (1,)), ((), ())), preferred_element_type=jnp.float32)
        s = jnp.maximum(s, 0.0)
        score = w_cols[0] * s[0:rows]
        for h in range(1, N_IDX_HEADS):
            score = score + w_cols[h] * s[h * rows:(h + 1) * rows]
        visible = (c * SAMPLE_CHUNK + lane) <= (past_len + q_row)
        key_ref[c] = jnp.where(visible, _sortable_key(score), NEG_INF_KEY)
        return carry

    lax.fori_loop(0, n_chunks, score_chunk, 0)

    def count(pred):
        def body(c, acc):
            return acc + pred(key_ref[c]).astype(jnp.float32)
        acc = lax.fori_loop(0, n_chunks, body, jnp.zeros((rows, SAMPLE_CHUNK), jnp.float32))
        return jnp.sum(acc, axis=-1, keepdims=True)

    kf = float(topk)
    t0 = jnp.where(count(lambda k: k >= 0) >= kf, 0, INT_MIN).astype(jnp.int32)

    def search(it, t):
        cand = t + (jnp.int32(1) << (30 - it))
        return jnp.where(count(lambda k: k >= cand) >= kf, cand, t)

    thr = lax.fori_loop(0, 31, search, t0)
    n_take_eq = kf - count(lambda k: k > thr)

    wait_pages(1)
    wait_pages(2)

    head_of_lane = lax.broadcasted_iota(jnp.int32, (rows, GROUP_W), 1) // HEAD_DIM
    q = q_ref[...]
    q_heads = jnp.concatenate([jnp.where(head_of_lane == h, q, jnp.zeros_like(q)) for h in range(N_HEADS)], axis=0)
    tri = (lax.broadcasted_iota(jnp.int32, (SAMPLE_CHUNK, SAMPLE_CHUNK), 0)
           < lax.broadcasted_iota(jnp.int32, (SAMPLE_CHUNK, SAMPLE_CHUNK), 1)).astype(jnp.bfloat16)

    m_ref[...] = jnp.full(m_ref.shape, -jnp.inf, jnp.float32)
    l_ref[...] = jnp.zeros(l_ref.shape, jnp.float32)
    acc_ref[...] = jnp.zeros(acc_ref.shape, jnp.float32)

    def attend_chunk(c, eq_before):
        p0 = pl.multiple_of(c * PAGES_PER_CHUNK, PAGES_PER_CHUNK)
        keys = key_ref[c]
        eq = keys == thr
        eq_excl = jnp.dot(eq.astype(jnp.bfloat16), tri, preferred_element_type=jnp.float32)
        sel = (keys > thr) | (eq & ((eq_before + eq_excl) < n_take_eq))
        sel = sel & (keys != NEG_INF_KEY)
        sel_all = jnp.concatenate([sel] * N_HEADS, axis=0)
        kc = k_buf[pl.ds(p0, PAGES_PER_CHUNK)].reshape(SAMPLE_CHUNK, GROUP_W).astype(jnp.bfloat16)
        vc = v_buf[pl.ds(p0, PAGES_PER_CHUNK)].reshape(SAMPLE_CHUNK, GROUP_W).astype(jnp.bfloat16)
        lg = lax.dot_general(q_heads, kc, (((1,), (1,)), ((), ())),
                             preferred_element_type=jnp.float32) * (HEAD_DIM ** -0.5)
        lg = jnp.where(sel_all, lg, -jnp.inf)
        m_old = m_ref[...]
        m_new = jnp.maximum(m_old, jnp.max(lg, axis=-1, keepdims=True))
        m_safe = jnp.where(m_new == -jnp.inf, 0.0, m_new)
        p = jnp.exp(lg - m_safe)
        alpha = jnp.exp(m_old - m_safe)
        l_ref[...] = alpha * l_ref[...] + jnp.sum(p, axis=-1, keepdims=True)
        m_ref[...] = m_new
        acc_ref[...] = alpha * acc_ref[...] + jnp.dot(p.astype(jnp.bfloat16), vc, preferred_element_type=jnp.float32)
        return eq_before + jnp.sum(eq.astype(jnp.float32), axis=-1, keepdims=True)

    lax.fori_loop(0, n_chunks, attend_chunk, jnp.zeros((rows, 1), jnp.float32))

    res = acc_ref[...] / l_ref[...]
    out = jnp.zeros((rows, GROUP_W), jnp.float32)
    for h in range(N_HEADS):
        out = jnp.where(head_of_lane == h, res[h * rows:(h + 1) * rows], out)
    o_ref[...] = out


def dsa_sample(q, k, v, q_idx, k_idx, w_idx, cache_k, cache_v, cache_kidx, page_table, layer):
    db, t, _ = q.shape
    n_pages = page_table.shape[1]
    depth, n_pool = cache_k.shape[:2]
    topk = min(TOPK_MAX, (n_pages * PAGE_SIZE + t) // 4)
    rows = SAMPLE_ROWS
    n_chunks = pl.cdiv(n_pages + 1, PAGES_PER_CHUNK)
    n_buf_pages = n_chunks * PAGES_PER_CHUNK
    bf = jnp.bfloat16
    pad_rows = lambda a, r: jnp.pad(a, [(0, 0), (0, r - t), (0, 0)])
    qi = pad_rows(q_idx, rows).astype(bf).reshape(db, rows, N_IDX_HEADS, IDX_DIM).transpose(0, 2, 1, 3)
    per_b = lambda *shape: pl.BlockSpec((None,) + shape, lambda b, pt: (b,) + (0,) * len(shape))
    hbm = pl.BlockSpec(memory_space=pl.ANY)
    grid_spec = pltpu.PrefetchScalarGridSpec(
        num_scalar_prefetch=1,
        grid=(db,),
        in_specs=[per_b(rows, GROUP_W), per_b(N_IDX_HEADS, rows, IDX_DIM), per_b(rows, N_IDX_HEADS),
                  per_b(PAGE_SIZE, IDX_DIM), per_b(PAGE_SIZE, GROUP_W), per_b(PAGE_SIZE, GROUP_W),
                  hbm, hbm, hbm],
        out_specs=per_b(rows, GROUP_W),
        scratch_shapes=[
            pltpu.VMEM((n_buf_pages, PAGE_SIZE, IDX_DIM), jnp.float32),
            pltpu.VMEM((n_buf_pages, PAGE_SIZE, GROUP_W), jnp.float32),
            pltpu.VMEM((n_buf_pages, PAGE_SIZE, GROUP_W), jnp.float32),
            pltpu.VMEM((n_chunks, rows, SAMPLE_CHUNK), jnp.int32),
            pltpu.VMEM((N_HEADS * rows, 1), jnp.float32),
            pltpu.VMEM((N_HEADS * rows, 1), jnp.float32),
            pltpu.VMEM((N_HEADS * rows, GROUP_W), jnp.float32),
            pltpu.SemaphoreType.DMA((3,)),
        ],
    )
    out = pl.pallas_call(
        functools.partial(_dsa_sample_kernel, layer=layer, n_pages=n_pages, topk=topk),
        grid_spec=grid_spec,
        out_shape=jax.ShapeDtypeStruct((db, rows, GROUP_W), jnp.float32),
        compiler_params=_compiler_params(),
        name="dsa_sample",
    )(page_table, pad_rows(q, rows).astype(bf), qi, pad_rows(w_idx, rows),
      pad_rows(k_idx, PAGE_SIZE), pad_rows(k, PAGE_SIZE), pad_rows(v, PAGE_SIZE),
      cache_kidx, cache_k.reshape(depth, n_pool, PAGE_SIZE, GROUP_W), cache_v.reshape(depth, n_pool, PAGE_SIZE, GROUP_W))
    return out[:, :t]


def layer_norm(x, g, b):
    mu = jnp.mean(x, axis=-1, keepdims=True)
    var = jnp.mean(jnp.square(x - mu), axis=-1, keepdims=True)
    return (x - mu) * lax.rsqrt(var + LN_EPS) * g + b


def causal_dwconv(x_ext, w):
    c = x_ext.shape[-1]
    return lax.conv_general_dilated(x_ext, w[:, None, :], window_strides=(1,), padding='VALID',
                                    dimension_numbers=('NWC', 'WIO', 'NWC'), feature_group_count=c)


def multiscale_pool(x_ext, n_ctx, start_pos):
    b, l, _ = x_ext.shape
    t = l - n_ctx
    cs = jnp.concatenate([jnp.zeros((b, 1, GROUP_W), jnp.float32), jnp.cumsum(x_ext, axis=1)], axis=1)
    rows = jnp.arange(n_ctx, l)
    pos = start_pos + jnp.arange(t)
    outs = []
    for gi, w in enumerate(POOL_WINDOWS):
        csg = cs[..., gi * POOL_GROUP:(gi + 1) * POOL_GROUP]
        lo = jnp.maximum(rows + 1 - w, 0)
        cnt = jnp.minimum(pos + 1, w).astype(jnp.float32)
        outs.append((csg[:, rows + 1] - csg[:, lo]) / cnt[None, :, None])
    mean = jnp.concatenate(outs, axis=-1)
    return mean - x_ext[:, n_ctx:]


def mixer_block(z, lw, past, start_pos):
    b, t, _ = z.shape
    parts, start = [], 0
    for size in SPLIT_SIZES:
        parts.append(z[..., start:start + size])
        start += size
    (a_val, a_gate, b_gate, c_gate, b_in, q, k, v, q_idx, k_idx, w_idx, pool_in) = parts
    u = a_val * jax.nn.sigmoid(a_gate)
    ctx_a = jnp.zeros((b, CONV_A_WIDTH - 1, GROUP_W), u.dtype) if past is None else past['conv_a']
    u_ext = jnp.concatenate([ctx_a, u], axis=1)
    y_a = causal_dwconv(u_ext, lw['conv_a_w']) + lw['conv_a_b']
    y_a = jax.nn.silu(layer_norm(y_a, lw['ln_a_g'], lw['ln_a_b']))
    g_in = c_gate * b_in
    ctx_b = jnp.zeros((b, CONV_B_WIDTH - 1, GROUP_W), g_in.dtype) if past is None else past['conv_b']
    g_ext = jnp.concatenate([ctx_b, g_in], axis=1)
    y_b = b_gate * causal_dwconv(g_ext, lw['conv_b_w'])
    if past is None:
        y_c = dsa_prompt(q, k, v, q_idx, k_idx, w_idx, min(TOPK_MAX, t // 4))
    else:
        y_c = dsa_sample(q, k, v, q_idx, k_idx, w_idx, past['cache_k'], past['cache_v'], past['cache_kidx'],
                         past['page_table'], past['layer'])
    if past is None:
        p_ext, n_ctx = pool_in, 0
    else:
        p_ext, n_ctx = jnp.concatenate([past['pool'], pool_in], axis=1), POOL_CTX
    pooled = multiscale_pool(p_ext, n_ctx, start_pos).reshape(b, t, len(POOL_WINDOWS), POOL_GROUP)
    y_d = jnp.einsum('btgc,gcd->btgd', pooled, lw['pool_w']).reshape(b, t, GROUP_W) * lw['pool_scale']
    new = (k.reshape(b, t, N_HEADS, HEAD_DIM), v.reshape(b, t, N_HEADS, HEAD_DIM), k_idx,
           u_ext[:, -(CONV_A_WIDTH - 1):], g_ext[:, -(CONV_B_WIDTH - 1):], p_ext[:, -POOL_CTX:])
    return (y_a, y_b, y_c, y_d), new


def _top_rows(s, k):
    r = s.shape[0]
    row = lax.broadcasted_iota(jnp.int32, s.shape, 0)
    vals, idxs = [], []
    for _ in range(k):
        m = jnp.max(s, axis=0, keepdims=True)
        idx = jnp.min(jnp.where(s == m, row, r), axis=0, keepdims=True)
        s = jnp.where(row == idx, -jnp.inf, s)
        vals.append(m)
        idxs.append(idx)
    return jnp.concatenate(vals, axis=0), jnp.concatenate(idxs, axis=0)


def _pick_rows(table, which):
    row = lax.broadcasted_iota(jnp.int32, table.shape, 0)
    out = []
    for j in range(which.shape[0]):
        out.append(jnp.sum(jnp.where(row == which[j:j + 1], table, 0), axis=0, keepdims=True))
    return jnp.concatenate(out, axis=0)


def _peer_route_kernel(h_ref, g_ref, wqt_ref, sk_ref, xn_ref, k1_ref, k2_ref, gate_ref,
                       qt_ref, k1t_ref, k2t_ref, gt_ref):
    x = h_ref[...]
    xn = (x * lax.rsqrt(jnp.mean(x * x, axis=-1, keepdims=True) + RMS_EPS) * g_ref[...]).astype(jnp.bfloat16)
    xn_ref[...] = xn
    qt_ref[...] = lax.dot_general(wqt_ref[...], xn, (((1,), (1,)), ((), ())),
                                  preferred_element_type=jnp.float32).astype(jnp.bfloat16)

    def head(h, carry):
        r0 = pl.multiple_of(h * 2 * PEER_HALF, 2 * PEER_HALF)
        s1 = jnp.dot(sk_ref[2 * h], qt_ref[pl.ds(r0, PEER_HALF), :], preferred_element_type=jnp.float32)
        s2 = jnp.dot(sk_ref[2 * h + 1], qt_ref[pl.ds(r0 + PEER_HALF, PEER_HALF), :],
                     preferred_element_type=jnp.float32)
        v1, i1 = _top_rows(s1, PEER_TOPK)
        v2, i2 = _top_rows(s2, PEER_TOPK)
        cand = jnp.concatenate([v1[a:a + 1] + v2 for a in range(PEER_TOPK)], axis=0)
        best, flat = _top_rows(cand, PEER_TOPK)
        e = jnp.exp(best - best[0:1])
        o0 = pl.multiple_of(h * PEER_TOPK, PEER_TOPK)
        k1t_ref[pl.ds(o0, PEER_TOPK), :] = _pick_rows(i1, flat >> 4)
        k2t_ref[pl.ds(o0, PEER_TOPK), :] = _pick_rows(i2, flat & (PEER_TOPK - 1))
        gt_ref[pl.ds(o0, PEER_TOPK), :] = e / jnp.sum(e, axis=0, keepdims=True)
        return carry

    lax.fori_loop(0, PEER_HEADS, head, 0)
    k1_ref[...] = k1t_ref[...].T
    k2_ref[...] = k2t_ref[...].T
    gate_ref[...] = gt_ref[...].T


def peer_route(h, g, wqt, sk):
    n, d = h.shape
    tm = ROUTE_TILE
    row_spec = pl.BlockSpec((tm, N_ROUTES), lambda i: (i, 0))
    return pl.pallas_call(
        _peer_route_kernel,
        grid=(n // tm,),
        in_specs=[pl.BlockSpec((tm, d), lambda i: (i, 0)),
                  pl.BlockSpec((1, d), lambda i: (0, 0)),
                  pl.BlockSpec(wqt.shape, lambda i: (0, 0)),
                  pl.BlockSpec(sk.shape, lambda i: (0, 0, 0))],
        out_specs=[pl.BlockSpec((tm, d), lambda i: (i, 0)), row_spec, row_spec, row_spec],
        out_shape=[jax.ShapeDtypeStruct((n, d), jnp.bfloat16),
                   jax.ShapeDtypeStruct((n, N_ROUTES), jnp.int32),
                   jax.ShapeDtypeStruct((n, N_ROUTES), jnp.int32),
                   jax.ShapeDtypeStruct((n, N_ROUTES), jnp.float32)],
        scratch_shapes=[pltpu.VMEM((PEER_HEADS * 2 * PEER_HALF, tm), jnp.bfloat16),
                        pltpu.VMEM((N_ROUTES, tm), jnp.int32),
                        pltpu.VMEM((N_ROUTES, tm), jnp.int32),
                        pltpu.VMEM((N_ROUTES, tm), jnp.float32)],
        compiler_params=_compiler_params(),
        name="peer_route",
    )(h, g.reshape(1, d), wqt, sk)


def _peer_eval_kernel(xn_ref, k1_ref, k2_ref, gate_ref, u_ref, v_ref, h_ref, o_ref, w_ref):
    j = pl.program_id(1)
    tn = xn_ref.shape[0]

    @pl.when(j == 0)
    def _():
        sub = lax.broadcasted_iota(jnp.int32, (N_KEYS, N_ROUTES), 0)

        def token(n, carry):
            k1 = jnp.broadcast_to(k1_ref[pl.ds(n, 1), :], (N_KEYS, N_ROUTES))
            k2 = jnp.broadcast_to(k2_ref[pl.ds(n, 1), :], (N_KEYS, N_ROUTES))
            gt = jnp.broadcast_to(gate_ref[pl.ds(n, 1), :], (N_KEYS, N_ROUTES))
            left = jnp.where(sub == k1, gt, 0.0)
            left_hi = left.astype(jnp.bfloat16)
            left_lo = (left - left_hi.astype(jnp.float32)).astype(jnp.bfloat16)
            right = (sub == k2).astype(jnp.bfloat16)
            nt = (((1,), (1,)), ((), ()))
            tile = (lax.dot_general(left_hi, right, nt, preferred_element_type=jnp.float32)
                    + lax.dot_general(left_lo, right, nt, preferred_element_type=jnp.float32))
            w_ref[pl.ds(pl.multiple_of(n * N_KEYS, N_KEYS), N_KEYS), :] = tile
            return carry

        lax.fori_loop(0, tn, token, 0, unroll=GATE_BUILD_UNROLL)
        o_ref[...] = h_ref[...]

    act = lax.dot_general(xn_ref[...], u_ref[...], (((1,), (1,)), ((), ())), preferred_element_type=jnp.float32)
    wgt = jnp.concatenate([w_ref[pl.ds(j * KEYS_PER_STEP + c, tn, stride=N_KEYS), :]
                           for c in range(KEYS_PER_STEP)], axis=1)
    coef = (wgt * jax.nn.gelu(act)).astype(jnp.bfloat16)
    o_ref[...] += jnp.dot(coef, v_ref[...], preferred_element_type=jnp.float32)


def peer_eval(xn, k1, k2, gate, u, v, h):
    n, d = xn.shape
    tn = PEER_TILE
    eb = KEYS_PER_STEP * N_KEYS
    tok = lambda i, j: (i, 0)
    return pl.pallas_call(
        _peer_eval_kernel,
        grid=(n // tn, u.shape[0] // eb),
        in_specs=[pl.BlockSpec((tn, d), tok),
                  pl.BlockSpec((tn, N_ROUTES), tok), pl.BlockSpec((tn, N_ROUTES), tok),
                  pl.BlockSpec((tn, N_ROUTES), tok),
                  pl.BlockSpec((eb, d), lambda i, j: (j, 0)),
                  pl.BlockSpec((eb, d), lambda i, j: (j, 0)),
                  pl.BlockSpec((tn, d), tok)],
        out_specs=pl.BlockSpec((tn, d), tok),
        out_shape=jax.ShapeDtypeStruct((n, d), jnp.float32),
        scratch_shapes=[pltpu.VMEM((tn * N_KEYS, N_KEYS), jnp.float32)],
        compiler_params=_compiler_params(dimension_semantics=("arbitrary", "arbitrary")),
        name="peer_eval",
    )(xn, k1, k2, gate, u, v, h)


def kernel(x_prompt, x_sample, cache_k, cache_v, cache_kidx, state_conv_a, state_conv_b, state_pool,
           page_table, meta_tokens, w_in, conv_a_w, conv_a_b, ln_a_g, ln_a_b, conv_b_w, pool_w,
           pool_scale, w_out, norm_mix_g, norm_ffn_g, peer_wq, peer_subkeys, peer_u, peer_v, norm_final_g):
    b, seq, d = x_prompt.shape
    db, dt, _ = x_sample.shape
    depth = w_in.shape[0]
    tp = N_META + seq
    n_p, n_s = b * tp, db * dt
    n_tok = n_p + n_s
    n_pad = pl.cdiv(n_tok, PEER_TILE) * PEER_TILE
    bf = jnp.bfloat16
    past_len = page_table.shape[1] * PAGE_SIZE

    meta = jnp.broadcast_to(meta_tokens[None], (b, N_META, d))
    hp = jnp.concatenate([meta, x_prompt], axis=1).reshape(n_p, d)
    h = jnp.concatenate([hp, x_sample.reshape(n_s, d), jnp.zeros((n_pad - n_tok, d), jnp.float32)], axis=0)

    news_p, news_s = [], []
    for l in range(depth):
        lw = {'conv_a_w': conv_a_w[l], 'conv_a_b': conv_a_b[l], 'ln_a_g': ln_a_g[l], 'ln_a_b': ln_a_b[l],
              'conv_b_w': conv_b_w[l], 'pool_w': pool_w[l], 'pool_scale': pool_scale[l]}
        past = {'cache_k': cache_k, 'cache_v': cache_v, 'cache_kidx': cache_kidx, 'layer': l,
                'conv_a': state_conv_a[l], 'conv_b': state_conv_b[l], 'pool': state_pool[l],
                'page_table': page_table}
        z = norm_matmul(h, norm_mix_g[l], w_in[l].astype(bf))
        ys_p, new_p = mixer_block(z[:n_p].reshape(b, tp, N_COLS), lw, None, 0)
        ys_s, new_s = mixer_block(z[n_p:n_tok].reshape(db, dt, N_COLS), lw, past, past_len)
        ys = [jnp.concatenate([yp.reshape(n_p, GROUP_W), ysm.reshape(n_s, GROUP_W),
                               jnp.zeros((n_pad - n_tok, GROUP_W), jnp.float32)], axis=0)
              for yp, ysm in zip(ys_p, ys_s)]
        h = out_proj(h, ys, w_out[l].astype(bf))
        xn, k1, k2, gate = peer_route(h, norm_ffn_g[l], peer_wq[l].T.astype(bf),
                                      peer_subkeys[l].reshape(PEER_HEADS * 2, N_KEYS, PEER_HALF).astype(bf))
        h = peer_eval(xn, k1, k2, gate, peer_u[l].astype(bf), peer_v[l].astype(bf), h)
        news_p.append(new_p)
        news_s.append(new_s)

    y = rms_norm_rows(h, norm_final_g)
    y_prompt = y[:n_p].reshape(b, tp, d)[:, N_META:]
    y_sample = y[n_p:n_tok].reshape(db, dt, d)
    outs_p = [jnp.stack([n[i] for n in news_p]) for i in range(6)]
    outs_s = [jnp.stack([n[i] for n in news_s]) for i in range(6)]
    return (y_prompt, y_sample, *outs_p, *outs_s)
```

```python
import functools

import jax
import jax.numpy as jnp
from jax import lax
from jax.experimental import pallas as pl
from jax.experimental.pallas import tpu as pltpu

D_MODEL = 1024
N_META = 16
GROUP_W = 256
CONV_A_WIDTH = 31
CONV_B_WIDTH = 3
N_HEADS = 4
HEAD_DIM = 64
N_IDX_HEADS = 8
IDX_DIM = 64
TOPK_MAX = 256
PAGE_SIZE = 128
POOL_WINDOWS = (2, 4, 8, 16)
POOL_GROUP = 64
POOL_CTX = 15
PEER_HEADS = 8
N_KEYS = 128
PEER_HALF = 128
PEER_TOPK = 16
N_ROUTES = PEER_HEADS * PEER_TOPK
RMS_EPS = 1e-6
LN_EPS = 1e-5
SPLIT_SIZES = (GROUP_W,) * 8 + (N_IDX_HEADS * IDX_DIM, IDX_DIM, N_IDX_HEADS, GROUP_W)
N_COLS = sum(SPLIT_SIZES)

Q_BLOCK = 128
KEY_CHUNK = 256
ROW_TILE = 256
ROUTE_TILE = 256
PEER_TILE = 512
HALF_KEYS = N_KEYS // 2
KEY_PAIRS_PER_STEP = 4
GATE_BUILD_UNROLL = 16
MIX_COLS = 6 * GROUP_W
MIX_TILE = 512
MIX_HALO = 32
SAMPLE_ROWS = 8
PAGES_PER_CHUNK = 4
SAMPLE_CHUNK = PAGES_PER_CHUNK * PAGE_SIZE
INT_MIN = -2 ** 31
NEG_INF_KEY = -2139095041
VMEM_LIMIT_BYTES = 48 * 1024 * 1024
PEER_VMEM_LIMIT_BYTES = 56 * 1024 * 1024


def _compiler_params(vmem_limit_bytes=VMEM_LIMIT_BYTES, **kw):
    return pltpu.CompilerParams(vmem_limit_bytes=vmem_limit_bytes, **kw)


def _norm_matmul_kernel(x_ref, g_ref, w_ref, o_ref):
    x = x_ref[...]
    y = x * lax.rsqrt(jnp.mean(x * x, axis=-1, keepdims=True) + RMS_EPS) * g_ref[...]
    o_ref[...] = jnp.dot(y.astype(jnp.bfloat16), w_ref[...], preferred_element_type=jnp.float32)


def norm_matmul(x, g, w):
    n, d = x.shape
    c = w.shape[1]
    return pl.pallas_call(
        _norm_matmul_kernel,
        grid=(n // ROW_TILE,),
        in_specs=[pl.BlockSpec((ROW_TILE, d), lambda i: (i, 0)),
                  pl.BlockSpec((1, d), lambda i: (0, 0)),
                  pl.BlockSpec((d, c), lambda i: (0, 0))],
        out_specs=pl.BlockSpec((ROW_TILE, c), lambda i: (i, 0)),
        out_shape=jax.ShapeDtypeStruct((n, c), jnp.float32),
        compiler_params=_compiler_params(),
        name="norm_matmul",
    )(x, g.reshape(1, d), w)


def _out_proj_kernel(h_ref, ya_ref, yb_ref, yc_ref, yd_ref, w_ref, o_ref):
    acc = h_ref[...]
    for gi, y_ref in enumerate((ya_ref, yb_ref, yc_ref, yd_ref)):
        acc += jnp.dot(y_ref[...].astype(jnp.bfloat16), w_ref[gi * GROUP_W:(gi + 1) * GROUP_W, :],
                       preferred_element_type=jnp.float32)
    o_ref[...] = acc


def out_proj(h, ys, w):
    n, d = h.shape
    y_spec = pl.BlockSpec((ROW_TILE, GROUP_W), lambda i: (i, 0))
    return pl.pallas_call(
        _out_proj_kernel,
        grid=(n // ROW_TILE,),
        in_specs=[pl.BlockSpec((ROW_TILE, d), lambda i: (i, 0)), y_spec, y_spec, y_spec, y_spec,
                  pl.BlockSpec(w.shape, lambda i: (0, 0))],
        out_specs=pl.BlockSpec((ROW_TILE, d), lambda i: (i, 0)),
        out_shape=jax.ShapeDtypeStruct((n, d), jnp.float32),
        compiler_params=_compiler_params(),
        name="out_proj",
    )(h, *ys, w)


def _rms_norm_kernel(x_ref, g_ref, o_ref):
    x = x_ref[...]
    o_ref[...] = x * lax.rsqrt(jnp.mean(x * x, axis=-1, keepdims=True) + RMS_EPS) * g_ref[...]


def rms_norm_rows(x, g):
    n, d = x.shape
    return pl.pallas_call(
        _rms_norm_kernel,
        grid=(n // ROW_TILE,),
        in_specs=[pl.BlockSpec((ROW_TILE, d), lambda i: (i, 0)), pl.BlockSpec((1, d), lambda i: (0, 0))],
        out_specs=pl.BlockSpec((ROW_TILE, d), lambda i: (i, 0)),
        out_shape=jax.ShapeDtypeStruct((n, d), jnp.float32),
        compiler_params=_compiler_params(),
        name="final_norm",
    )(x, g.reshape(1, d))


def _sortable_key(x):
    x = jnp.where(x == 0.0, 0.0, x)
    b = pltpu.bitcast(x, jnp.int32)
    return b ^ ((b >> 31) & 0x7FFFFFFF)


def _dsa_prompt_kernel(q_ref, qi_ref, w_ref, kidx_ref, k_ref, v_ref, o_ref,
                       key_ref, m_ref, l_ref, acc_ref, *, topk):
    i = pl.program_id(1)
    n_chunks = (i + 2) // 2
    q_pos = i * Q_BLOCK + lax.broadcasted_iota(jnp.int32, (Q_BLOCK, KEY_CHUNK), 0)
    lane = lax.broadcasted_iota(jnp.int32, (Q_BLOCK, KEY_CHUNK), 1)
    qi = qi_ref[...].reshape(N_IDX_HEADS * Q_BLOCK, IDX_DIM)
    w = w_ref[...]
    w_cols = [jnp.broadcast_to(w[:, h:h + 1], (Q_BLOCK, KEY_CHUNK)) for h in range(N_IDX_HEADS)]

    def score_chunk(c, carry):
        off = pl.multiple_of(c * KEY_CHUNK, KEY_CHUNK)
        kc = kidx_ref[pl.ds(off, KEY_CHUNK), :]
        s = lax.dot_general(qi, kc, (((1,), (1,)), ((), ())), preferred_element_type=jnp.float32)
        s = jnp.maximum(s, 0.0)
        score = w_cols[0] * s[0:Q_BLOCK]
        for h in range(1, N_IDX_HEADS):
            score = score + w_cols[h] * s[h * Q_BLOCK:(h + 1) * Q_BLOCK]
        visible = (off + lane) <= q_pos
        key_ref[c] = jnp.where(visible, _sortable_key(score), NEG_INF_KEY)
        return carry

    lax.fori_loop(0, n_chunks, score_chunk, 0)

    def count(pred):
        def body(c, acc):
            p = pred(key_ref[c]).astype(jnp.float32)
            return acc + p[:, :128] + p[:, 128:]
        acc = lax.fori_loop(0, n_chunks, body, jnp.zeros((Q_BLOCK, 128), jnp.float32))
        return jnp.sum(acc, axis=-1, keepdims=True)

    kf = float(topk)
    t0 = jnp.where(count(lambda k: k >= 0) >= kf, 0, INT_MIN).astype(jnp.int32)

    def search(it, t):
        cand = t + (jnp.int32(1) << (30 - it))
        return jnp.where(count(lambda k: k >= cand) >= kf, cand, t)

    thr = lax.fori_loop(0, 31, search, t0)
    n_take_eq = kf - count(lambda k: k > thr)

    head_of_lane = lax.broadcasted_iota(jnp.int32, (Q_BLOCK, GROUP_W), 1) // HEAD_DIM
    q = q_ref[...]
    q_heads = jnp.concatenate([jnp.where(head_of_lane == h, q, jnp.zeros_like(q)) for h in range(N_HEADS)], axis=0)
    tri = (lax.broadcasted_iota(jnp.int32, (KEY_CHUNK, KEY_CHUNK), 0)
           < lax.broadcasted_iota(jnp.int32, (KEY_CHUNK, KEY_CHUNK), 1)).astype(jnp.bfloat16)

    m_ref[...] = jnp.full(m_ref.shape, -jnp.inf, jnp.float32)
    l_ref[...] = jnp.zeros(l_ref.shape, jnp.float32)
    acc_ref[...] = jnp.zeros(acc_ref.shape, jnp.float32)

    def attend_chunk(c, eq_before):
        off = pl.multiple_of(c * KEY_CHUNK, KEY_CHUNK)
        keys = key_ref[c]
        eq = keys == thr
        eq_excl = jnp.dot(eq.astype(jnp.bfloat16), tri, preferred_element_type=jnp.float32)
        sel = (keys > thr) | (eq & ((eq_before + eq_excl) < n_take_eq))
        sel = sel & (keys != NEG_INF_KEY)
        sel_all = jnp.concatenate([sel] * N_HEADS, axis=0)
        kc = k_ref[pl.ds(off, KEY_CHUNK), :]
        vc = v_ref[pl.ds(off, KEY_CHUNK), :]
        lg = lax.dot_general(q_heads, kc, (((1,), (1,)), ((), ())),
                             preferred_element_type=jnp.float32) * (HEAD_DIM ** -0.5)
        lg = jnp.where(sel_all, lg, -jnp.inf)
        m_old = m_ref[...]
        m_new = jnp.maximum(m_old, jnp.max(lg, axis=-1, keepdims=True))
        m_safe = jnp.where(m_new == -jnp.inf, 0.0, m_new)
        p = jnp.exp(lg - m_safe)
        alpha = jnp.exp(m_old - m_safe)
        l_ref[...] = alpha * l_ref[...] + jnp.sum(p, axis=-1, keepdims=True)
        m_ref[...] = m_new
        acc_ref[...] = alpha * acc_ref[...] + jnp.dot(p.astype(jnp.bfloat16), vc, preferred_element_type=jnp.float32)
        return eq_before + jnp.sum(eq.astype(jnp.float32), axis=-1, keepdims=True)

    lax.fori_loop(0, n_chunks, attend_chunk, jnp.zeros((Q_BLOCK, 1), jnp.float32))

    res = acc_ref[...] / l_ref[...]
    out = jnp.zeros((Q_BLOCK, GROUP_W), jnp.float32)
    for h in range(N_HEADS):
        out = jnp.where(head_of_lane == h, res[h * Q_BLOCK:(h + 1) * Q_BLOCK], out)
    o_ref[...] = out


def dsa_prompt(q, k, v, q_idx, k_idx, w_idx, topk):
    b, t, _ = q.shape
    n_blk = pl.cdiv(t, Q_BLOCK)
    tq = n_blk * Q_BLOCK
    tk = pl.cdiv(tq, KEY_CHUNK) * KEY_CHUNK
    bf = jnp.bfloat16
    pad_q = lambda a: jnp.pad(a, [(0, 0), (0, tq - t), (0, 0)])
    pad_k = lambda a: jnp.pad(a, [(0, 0), (0, tk - t), (0, 0)])
    qi = pad_q(q_idx).astype(bf).reshape(b, tq, N_IDX_HEADS, IDX_DIM).transpose(0, 2, 1, 3)
    out = pl.pallas_call(
        functools.partial(_dsa_prompt_kernel, topk=topk),
        grid=(b, n_blk),
        in_specs=[
            pl.BlockSpec((None, Q_BLOCK, GROUP_W), lambda bi, i: (bi, i, 0)),
            pl.BlockSpec((None, N_IDX_HEADS, Q_BLOCK, IDX_DIM), lambda bi, i: (bi, 0, i, 0)),
            pl.BlockSpec((None, Q_BLOCK, N_IDX_HEADS), lambda bi, i: (bi, i, 0)),
            pl.BlockSpec((None, tk, IDX_DIM), lambda bi, i: (bi, 0, 0)),
            pl.BlockSpec((None, tk, GROUP_W), lambda bi, i: (bi, 0, 0)),
            pl.BlockSpec((None, tk, GROUP_W), lambda bi, i: (bi, 0, 0)),
        ],
        out_specs=pl.BlockSpec((None, Q_BLOCK, GROUP_W), lambda bi, i: (bi, i, 0)),
        out_shape=jax.ShapeDtypeStruct((b, tq, GROUP_W), jnp.float32),
        scratch_shapes=[
            pltpu.VMEM((tk // KEY_CHUNK, Q_BLOCK, KEY_CHUNK), jnp.int32),
            pltpu.VMEM((N_HEADS * Q_BLOCK, 1), jnp.float32),
            pltpu.VMEM((N_HEADS * Q_BLOCK, 1), jnp.float32),
            pltpu.VMEM((N_HEADS * Q_BLOCK, GROUP_W), jnp.float32),
        ],
        compiler_params=_compiler_params(),
        name="dsa_prompt",
    )(pad_q(q).astype(bf), qi, pad_q(w_idx), pad_k(k_idx).astype(bf), pad_k(k).astype(bf), pad_k(v).astype(bf))
    return out[:, :t]


def _dsa_sample_kernel(pt_ref, q_ref, qi_ref, w_ref, kin_ref, kn_ref, vn_ref, kidx_hbm, k_hbm, v_hbm, o_ref,
                       kidx_buf, k_buf, v_buf, key_ref, m_ref, l_ref, acc_ref, sems, *, layer, n_pages, topk):
    b = pl.program_id(0)
    n_chunks = key_ref.shape[0]
    n_buf_pages = kidx_buf.shape[0]
    rows = SAMPLE_ROWS

    def page_copy(p, which):
        src, dst = ((kidx_hbm, kidx_buf), (k_hbm, k_buf), (v_hbm, v_buf))[which]
        return pltpu.make_async_copy(src.at[layer, pt_ref[b, p]], dst.at[p], sems.at[which])

    def start_page(p, carry):
        for which in range(3):
            page_copy(p, which).start()
        return carry

    lax.fori_loop(0, n_pages, start_page, 0)

    for p in range(n_pages, n_buf_pages):
        kidx_buf[p] = kin_ref[...] if p == n_pages else jnp.zeros(kidx_buf.shape[1:], jnp.float32)
        k_buf[p] = kn_ref[...] if p == n_pages else jnp.zeros(k_buf.shape[1:], jnp.float32)
        v_buf[p] = vn_ref[...] if p == n_pages else jnp.zeros(v_buf.shape[1:], jnp.float32)

    def wait_pages(which):
        def body(p, carry):
            page_copy(p, which).wait()
            return carry
        lax.fori_loop(0, n_pages, body, 0)

    wait_pages(0)

    past_len = n_pages * PAGE_SIZE
    q_row = lax.broadcasted_iota(jnp.int32, (rows, SAMPLE_CHUNK), 0)
    lane = lax.broadcasted_iota(jnp.int32, (rows, SAMPLE_CHUNK), 1)
    qi = qi_ref[...].reshape(N_IDX_HEADS * rows, IDX_DIM)
    w = w_ref[...]
    w_cols = [jnp.broadcast_to(w[:, h:h + 1], (rows, SAMPLE_CHUNK)) for h in range(N_IDX_HEADS)]

    def score_chunk(c, carry):
        p0 = pl.multiple_of(c * PAGES_PER_CHUNK, PAGES_PER_CHUNK)
        kc = kidx_buf[pl.ds(p0, PAGES_PER_CHUNK)].reshape(SAMPLE_CHUNK, IDX_DIM).astype(jnp.bfloat16)
        s = lax.dot_general(qi, kc, (((1,), (1,)), ((), ())), preferred_element_type=jnp.float32)
        s = jnp.maximum(s, 0.0)
        score = w_cols[0] * s[0:rows]
        for h in range(1, N_IDX_HEADS):
            score = score + w_cols[h] * s[h * rows:(h + 1) * rows]
        visible = (c * SAMPLE_CHUNK + lane) <= (past_len + q_row)
        key_ref[c] = jnp.where(visible, _sortable_key(score), NEG_INF_KEY)
        return carry

    lax.fori_loop(0, n_chunks, score_chunk, 0)

    def count(pred):
        def body(c, acc):
            return acc + pred(key_ref[c]).astype(jnp.float32)
        acc = lax.fori_loop(0, n_chunks, body, jnp.zeros((rows, SAMPLE_CHUNK), jnp.float32))
        return jnp.sum(acc, axis=-1, keepdims=True)

    kf = float(topk)
    t0 = jnp.where(count(lambda k: k >= 0) >= kf, 0, INT_MIN).astype(jnp.int32)

    def search(it, t):
        cand = t + (jnp.int32(1) << (30 - it))
        return jnp.where(count(lambda k: k >= cand) >= kf, cand, t)

    thr = lax.fori_loop(0, 31, search, t0)
    n_take_eq = kf - count(lambda k: k > thr)

    wait_pages(1)
    wait_pages(2)

    head_of_lane = lax.broadcasted_iota(jnp.int32, (rows, GROUP_W), 1) // HEAD_DIM
    q = q_ref[...]
    q_heads = jnp.concatenate([jnp.where(head_of_lane == h, q, jnp.zeros_like(q)) for h in range(N_HEADS)], axis=0)
    tri = (lax.broadcasted_iota(jnp.int32, (SAMPLE_CHUNK, SAMPLE_CHUNK), 0)
           < lax.broadcasted_iota(jnp.int32, (SAMPLE_CHUNK, SAMPLE_CHUNK), 1)).astype(jnp.bfloat16)

    m_ref[...] = jnp.full(m_ref.shape, -jnp.inf, jnp.float32)
    l_ref[...] = jnp.zeros(l_ref.shape, jnp.float32)
    acc_ref[...] = jnp.zeros(acc_ref.shape, jnp.float32)

    def attend_chunk(c, eq_before):
        p0 = pl.multiple_of(c * PAGES_PER_CHUNK, PAGES_PER_CHUNK)
        keys = key_ref[c]
        eq = keys == thr
        eq_excl = jnp.dot(eq.astype(jnp.bfloat16), tri, preferred_element_type=jnp.float32)
        sel = (keys > thr) | (eq & ((eq_before + eq_excl) < n_take_eq))
        sel = sel & (keys != NEG_INF_KEY)
        sel_all = jnp.concatenate([sel] * N_HEADS, axis=0)
        kc = k_buf[pl.ds(p0, PAGES_PER_CHUNK)].reshape(SAMPLE_CHUNK, GROUP_W).astype(jnp.bfloat16)
        vc = v_buf[pl.ds(p0, PAGES_PER_CHUNK)].reshape(SAMPLE_CHUNK, GROUP_W).astype(jnp.bfloat16)
        lg = lax.dot_general(q_heads, kc, (((1,), (1,)), ((), ())),
                             preferred_element_type=jnp.float32) * (HEAD_DIM ** -0.5)
        lg = jnp.where(sel_all, lg, -jnp.inf)
        m_old = m_ref[...]
        m_new = jnp.maximum(m_old, jnp.max(lg, axis=-1, keepdims=True))
        m_safe = jnp.where(m_new == -jnp.inf, 0.0, m_new)
        p = jnp.exp(lg - m_safe)
        alpha = jnp.exp(m_old - m_safe)
        l_ref[...] = alpha * l_ref[...] + jnp.sum(p, axis=-1, keepdims=True)
        m_ref[...] = m_new
        acc_ref[...] = alpha * acc_ref[...] + jnp.dot(p.astype(jnp.bfloat16), vc, preferred_element_type=jnp.float32)
        return eq_before + jnp.sum(eq.astype(jnp.float32), axis=-1, keepdims=True)

    lax.fori_loop(0, n_chunks, attend_chunk, jnp.zeros((rows, 1), jnp.float32))

    res = acc_ref[...] / l_ref[...]
    out = jnp.zeros((rows, GROUP_W), jnp.float32)
    for h in range(N_HEADS):
        out = jnp.where(head_of_lane == h, res[h * rows:(h + 1) * rows], out)
    o_ref[...] = out


def dsa_sample(q, k, v, q_idx, k_idx, w_idx, cache_k, cache_v, cache_kidx, page_table, layer):
    db, t, _ = q.shape
    n_pages = page_table.shape[1]
    depth, n_pool = cache_k.shape[:2]
    topk = min(TOPK_MAX, (n_pages * PAGE_SIZE + t) // 4)
    rows = SAMPLE_ROWS
    n_chunks = pl.cdiv(n_pages + 1, PAGES_PER_CHUNK)
    n_buf_pages = n_chunks * PAGES_PER_CHUNK
    bf = jnp.bfloat16
    pad_rows = lambda a, r: jnp.pad(a, [(0, 0), (0, r - t), (0, 0)])
    qi = pad_rows(q_idx, rows).astype(bf).reshape(db, rows, N_IDX_HEADS, IDX_DIM).transpose(0, 2, 1, 3)
    per_b = lambda *shape: pl.BlockSpec((None,) + shape, lambda b, pt: (b,) + (0,) * len(shape))
    hbm = pl.BlockSpec(memory_space=pl.ANY)
    grid_spec = pltpu.PrefetchScalarGridSpec(
        num_scalar_prefetch=1,
        grid=(db,),
        in_specs=[per_b(rows, GROUP_W), per_b(N_IDX_HEADS, rows, IDX_DIM), per_b(rows, N_IDX_HEADS),
                  per_b(PAGE_SIZE, IDX_DIM), per_b(PAGE_SIZE, GROUP_W), per_b(PAGE_SIZE, GROUP_W),
                  hbm, hbm, hbm],
        out_specs=per_b(rows, GROUP_W),
        scratch_shapes=[
            pltpu.VMEM((n_buf_pages, PAGE_SIZE, IDX_DIM), jnp.float32),
            pltpu.VMEM((n_buf_pages, PAGE_SIZE, GROUP_W), jnp.float32),
            pltpu.VMEM((n_buf_pages, PAGE_SIZE, GROUP_W), jnp.float32),
            pltpu.VMEM((n_chunks, rows, SAMPLE_CHUNK), jnp.int32),
            pltpu.VMEM((N_HEADS * rows, 1), jnp.float32),
            pltpu.VMEM((N_HEADS * rows, 1), jnp.float32),
            pltpu.VMEM((N_HEADS * rows, GROUP_W), jnp.float32),
            pltpu.SemaphoreType.DMA((3,)),
        ],
    )
    out = pl.pallas_call(
        functools.partial(_dsa_sample_kernel, layer=layer, n_pages=n_pages, topk=topk),
        grid_spec=grid_spec,
        out_shape=jax.ShapeDtypeStruct((db, rows, GROUP_W), jnp.float32),
        compiler_params=_compiler_params(),
        name="dsa_sample",
    )(page_table, pad_rows(q, rows).astype(bf), qi, pad_rows(w_idx, rows),
      pad_rows(k_idx, PAGE_SIZE), pad_rows(k, PAGE_SIZE), pad_rows(v, PAGE_SIZE),
      cache_kidx, cache_k.reshape(depth, n_pool, PAGE_SIZE, GROUP_W), cache_v.reshape(depth, n_pool, PAGE_SIZE, GROUP_W))
    return out[:, :t]


def _mixer_streams(z):
    a_val, a_gate, b_gate, c_gate, b_in, pool_in = [z[:, g * GROUP_W:(g + 1) * GROUP_W] for g in range(6)]
    return a_val * jax.nn.sigmoid(a_gate), c_gate * b_in, pool_in, b_gate


def _mixers_kernel(z_ref, ctx_ref, wa_ref, ba_ref, lng_ref, lnb_ref, wb_ref, wp_ref, ps_ref,
                   ya_ref, yb_ref, yd_ref, ta_ref, tb_ref, tp_ref, ext_ref, *, ctx_from_z, start_pos, n_valid_last):
    i = pl.program_id(1)
    tt = z_ref.shape[0]
    u, g_in, p_in, b_gate = _mixer_streams(z_ref[...])
    if ctx_from_z:
        hu, hg, hp, _ = _mixer_streams(ctx_ref[...])
        hu, hg, hp = [jnp.where(i > 0, a, 0.0) for a in (hu, hg, hp)]
    else:
        hu, hg, hp = [ctx_ref[:, g * GROUP_W:(g + 1) * GROUP_W] for g in range(3)]
    for s, (head, body) in enumerate(((hu, u), (hg, g_in), (hp, p_in))):
        ext_ref[s, 0:MIX_HALO] = head
        ext_ref[s, MIX_HALO:MIX_HALO + tt] = body

    def rows(s, back):
        return ext_ref[s, pl.ds(MIX_HALO - back, tt), :]

    acc = jnp.zeros((tt, GROUP_W), jnp.float32)
    for j in range(CONV_A_WIDTH):
        acc = acc + wa_ref[j:j + 1, :] * rows(0, CONV_A_WIDTH - 1 - j)
    acc = acc + ba_ref[...]
    mu = jnp.mean(acc, axis=-1, keepdims=True)
    var = jnp.mean(jnp.square(acc - mu), axis=-1, keepdims=True)
    ln = (acc - mu) * lax.rsqrt(var + LN_EPS) * lng_ref[...] + lnb_ref[...]
    ya_ref[...] = ln * jax.nn.sigmoid(ln)

    cb = jnp.zeros((tt, GROUP_W), jnp.float32)
    for j in range(CONV_B_WIDTH):
        cb = cb + wb_ref[j:j + 1, :] * rows(1, CONV_B_WIDTH - 1 - j)
    yb_ref[...] = b_gate * cb

    lane_group = lax.broadcasted_iota(jnp.int32, (tt, GROUP_W), 1) // POOL_GROUP
    pos1 = start_pos + i * tt + lax.broadcasted_iota(jnp.int32, (tt, GROUP_W), 0) + 1
    run = p_in
    mean = jnp.zeros((tt, GROUP_W), jnp.float32)
    back = 1
    for gi, w in enumerate(POOL_WINDOWS):
        while back < w:
            run = run + rows(2, back)
            back += 1
        cnt = jnp.minimum(pos1, w).astype(jnp.float32)
        mean = jnp.where(lane_group == gi, run / cnt, mean)
    pooled = (mean - p_in).astype(jnp.bfloat16)
    yd_ref[...] = jnp.dot(pooled, wp_ref[...], preferred_element_type=jnp.float32) * ps_ref[...]

    @pl.when(i == pl.num_programs(1) - 1)
    def _():
        end = MIX_HALO + n_valid_last
        ta_ref[...] = ext_ref[0, end - (CONV_A_WIDTH - 1):end, :]
        tb_ref[...] = ext_ref[1, end - (CONV_B_WIDTH - 1):end, :]
        tp_ref[...] = ext_ref[2, end - POOL_CTX:end, :]


def mixers(z, ctx, lw, start_pos):
    b, t, _ = z.shape
    tt = min(MIX_TILE, t)
    n_tiles = pl.cdiv(t, tt)
    ctx_from_z = ctx is None
    if ctx_from_z:
        per_tile = tt // MIX_HALO
        ctx_arr = z
        ctx_spec = pl.BlockSpec((None, MIX_HALO, MIX_COLS), lambda bi, i: (bi, jnp.maximum(i * per_tile - 1, 0), 0))
    else:
        ctx_arr = ctx
        ctx_spec = pl.BlockSpec((None, MIX_HALO, 3 * GROUP_W), lambda bi, i: (bi, 0, 0))
    wp = jnp.zeros((GROUP_W, GROUP_W), jnp.float32)
    for gi in range(len(POOL_WINDOWS)):
        wp = wp.at[gi * POOL_GROUP:(gi + 1) * POOL_GROUP, gi * POOL_GROUP:(gi + 1) * POOL_GROUP].set(lw['pool_w'][gi])
    row = lambda a: a.reshape(1, GROUP_W)
    const = lambda shape: pl.BlockSpec(shape, lambda bi, i: (0,) * len(shape))
    y_spec = pl.BlockSpec((None, tt, GROUP_W), lambda bi, i: (bi, i, 0))
    tail = lambda r: pl.BlockSpec((None, r, GROUP_W), lambda bi, i: (bi, 0, 0))
    f32 = jnp.float32
    return pl.pallas_call(
        functools.partial(_mixers_kernel, ctx_from_z=ctx_from_z, start_pos=start_pos,
                          n_valid_last=t - (n_tiles - 1) * tt),
        grid=(b, n_tiles),
        in_specs=[pl.BlockSpec((None, tt, MIX_COLS), lambda bi, i: (bi, i, 0)), ctx_spec,
                  const((CONV_A_WIDTH, GROUP_W)), const((1, GROUP_W)), const((1, GROUP_W)), const((1, GROUP_W)),
                  const((CONV_B_WIDTH, GROUP_W)), const((GROUP_W, GROUP_W)), const((1, GROUP_W))],
        out_specs=[y_spec, y_spec, y_spec, tail(CONV_A_WIDTH - 1), tail(CONV_B_WIDTH - 1), tail(POOL_CTX)],
        out_shape=[jax.ShapeDtypeStruct((b, t, GROUP_W), f32)] * 3
                  + [jax.ShapeDtypeStruct((b, r, GROUP_W), f32) for r in (CONV_A_WIDTH - 1, CONV_B_WIDTH - 1, POOL_CTX)],
        scratch_shapes=[pltpu.VMEM((3, MIX_HALO + tt, GROUP_W), f32)],
        compiler_params=_compiler_params(),
        name="mixers",
    )(z, ctx_arr, lw['conv_a_w'], row(lw['conv_a_b']), row(lw['ln_a_g']), row(lw['ln_a_b']),
      lw['conv_b_w'], wp.astype(jnp.bfloat16), row(lw['pool_scale']))


def mixer_block(z, lw, past, start_pos):
    b, t, _ = z.shape
    parts, start = [], MIX_COLS
    for size in (GROUP_W, GROUP_W, GROUP_W, N_IDX_HEADS * IDX_DIM, IDX_DIM, N_IDX_HEADS):
        parts.append(z[..., start:start + size])
        start += size
    q, k, v, q_idx, k_idx, w_idx = parts
    if past is None:
        ctx = None
        y_c = dsa_prompt(q, k, v, q_idx, k_idx, w_idx, min(TOPK_MAX, t // 4))
    else:
        front = lambda a: jnp.pad(a, [(0, 0), (MIX_HALO - a.shape[1], 0), (0, 0)])
        ctx = jnp.concatenate([front(past['conv_a']), front(past['conv_b']), front(past['pool'])], axis=-1)
        y_c = dsa_sample(q, k, v, q_idx, k_idx, w_idx, past['cache_k'], past['cache_v'], past['cache_kidx'],
                         past['page_table'], past['layer'])
    y_a, y_b, y_d, tail_a, tail_b, tail_p = mixers(z, ctx, lw, start_pos)
    new = (k.reshape(b, t, N_HEADS, HEAD_DIM), v.reshape(b, t, N_HEADS, HEAD_DIM), k_idx, tail_a, tail_b, tail_p)
    return (y_a, y_b, y_c, y_d), new


def _top_rows(s, k):
    r = s.shape[0]
    row = lax.broadcasted_iota(jnp.int32, s.shape, 0)
    vals, idxs = [], []
    for _ in range(k):
        m = jnp.max(s, axis=0, keepdims=True)
        idx = jnp.min(jnp.where(s == m, row, r), axis=0, keepdims=True)
        s = jnp.where(row == idx, -jnp.inf, s)
        vals.append(m)
        idxs.append(idx)
    return jnp.concatenate(vals, axis=0), jnp.concatenate(idxs, axis=0)


def _pick_rows(table, which):
    row = lax.broadcasted_iota(jnp.int32, table.shape, 0)
    out = []
    for j in range(which.shape[0]):
        out.append(jnp.sum(jnp.where(row == which[j:j + 1], table, 0), axis=0, keepdims=True))
    return jnp.concatenate(out, axis=0)


def _cand_a(r):
    return jnp.where(r < 16, 0, jnp.where(r < 72, ((r - 16) >> 3) + 1, r - 64))


def _cand_b(r):
    return jnp.where(r < 16, r, jnp.where(r < 72, (r - 16) & 7, 0))


def _peer_route_kernel(h_ref, g_ref, wqt_ref, sk_ref, xn_ref, k1_ref, k2_ref, gate_ref,
                       qt_ref, k1t_ref, k2t_ref, gt_ref):
    x = h_ref[...]
    xn = (x * lax.rsqrt(jnp.mean(x * x, axis=-1, keepdims=True) + RMS_EPS) * g_ref[...]).astype(jnp.bfloat16)
    xn_ref[...] = xn
    qt_ref[...] = lax.dot_general(wqt_ref[...], xn, (((1,), (1,)), ((), ())),
                                  preferred_element_type=jnp.float32).astype(jnp.bfloat16)

    def head(h, carry):
        r0 = pl.multiple_of(h * 2 * PEER_HALF, 2 * PEER_HALF)
        s1 = jnp.dot(sk_ref[2 * h], qt_ref[pl.ds(r0, PEER_HALF), :], preferred_element_type=jnp.float32)
        s2 = jnp.dot(sk_ref[2 * h + 1], qt_ref[pl.ds(r0 + PEER_HALF, PEER_HALF), :],
                     preferred_element_type=jnp.float32)
        v1, i1 = _top_rows(s1, PEER_TOPK)
        v2, i2 = _top_rows(s2, PEER_TOPK)
        cand = jnp.concatenate([v1[0:1] + v2] + [v1[a:a + 1] + v2[0:8] for a in range(1, 8)] + [v1[8:16] + v2[0:1]],
                               axis=0)
        r = lax.broadcasted_iota(jnp.int32, cand.shape, 0)
        cand = jnp.where((_cand_a(r) + 1) * (_cand_b(r) + 1) <= PEER_TOPK, cand, -jnp.inf)
        best, ridx = _top_rows(cand, PEER_TOPK)
        e = jnp.exp(best - best[0:1])
        o0 = pl.multiple_of(h * PEER_TOPK, PEER_TOPK)
        k1t_ref[pl.ds(o0, PEER_TOPK), :] = _pick_rows(i1, _cand_a(ridx))
        k2t_ref[pl.ds(o0, PEER_TOPK), :] = _pick_rows(i2, _cand_b(ridx))
        gt_ref[pl.ds(o0, PEER_TOPK), :] = e / jnp.sum(e, axis=0, keepdims=True)
        return carry

    lax.fori_loop(0, PEER_HEADS, head, 0)
    k1_ref[...] = k1t_ref[...].T
    k2_ref[...] = k2t_ref[...].T
    gate_ref[...] = gt_ref[...].T


def peer_route(h, g, wqt, sk):
    n, d = h.shape
    tm = ROUTE_TILE
    row_spec = pl.BlockSpec((tm, N_ROUTES), lambda i: (i, 0))
    return pl.pallas_call(
        _peer_route_kernel,
        grid=(n // tm,),
        in_specs=[pl.BlockSpec((tm, d), lambda i: (i, 0)),
                  pl.BlockSpec((1, d), lambda i: (0, 0)),
                  pl.BlockSpec(wqt.shape, lambda i: (0, 0)),
                  pl.BlockSpec(sk.shape, lambda i: (0, 0, 0))],
        out_specs=[pl.BlockSpec((tm, d), lambda i: (i, 0)), row_spec, row_spec, row_spec],
        out_shape=[jax.ShapeDtypeStruct((n, d), jnp.bfloat16),
                   jax.ShapeDtypeStruct((n, N_ROUTES), jnp.int32),
                   jax.ShapeDtypeStruct((n, N_ROUTES), jnp.int32),
                   jax.ShapeDtypeStruct((n, N_ROUTES), jnp.float32)],
        scratch_shapes=[pltpu.VMEM((PEER_HEADS * 2 * PEER_HALF, tm), jnp.bfloat16),
                        pltpu.VMEM((N_ROUTES, tm), jnp.int32),
                        pltpu.VMEM((N_ROUTES, tm), jnp.int32),
                        pltpu.VMEM((N_ROUTES, tm), jnp.float32)],
        compiler_params=_compiler_params(),
        name="peer_route",
    )(h, g.reshape(1, d), wqt, sk)


def _peer_eval_kernel(xn_ref, k1_ref, k2_ref, gate_ref, u_ref, v_ref, h_ref, o_ref, w_ref):
    j = pl.program_id(1)
    tn = xn_ref.shape[0]

    @pl.when(j == 0)
    def _():
        sub = lax.broadcasted_iota(jnp.int32, (N_KEYS, N_ROUTES), 0)

        def token(n, carry):
            k1 = jnp.broadcast_to(k1_ref[pl.ds(n, 1), :], (N_KEYS, N_ROUTES))
            k2 = jnp.broadcast_to(k2_ref[pl.ds(n, 1), :], (N_KEYS, N_ROUTES))
            gt = jnp.broadcast_to(gate_ref[pl.ds(n, 1), :], (N_KEYS, N_ROUTES))
            left = jnp.where(sub == k1, gt, 0.0).astype(jnp.bfloat16)
            right = (sub == k2).astype(jnp.bfloat16)
            tile = lax.dot_general(left, right, (((1,), (1,)), ((), ())), preferred_element_type=jnp.float32)
            tile = tile.astype(jnp.bfloat16).astype(jnp.float32)
            lo = lax.shift_right_logical(pltpu.bitcast(tile[:HALF_KEYS], jnp.int32), 16)
            hi = pltpu.bitcast(tile[HALF_KEYS:], jnp.int32) & jnp.int32(-65536)
            w_ref[pl.ds(pl.multiple_of(n * HALF_KEYS, HALF_KEYS), HALF_KEYS), :] = hi | lo
            return carry

        lax.fori_loop(0, tn, token, 0, unroll=GATE_BUILD_UNROLL)
        o_ref[...] = h_ref[...]

    eb = KEY_PAIRS_PER_STEP * N_KEYS
    u_blk = u_ref[...].reshape(2 * eb, u_ref.shape[-1])
    v_blk = v_ref[...].reshape(2 * eb, v_ref.shape[-1])
    act = lax.dot_general(xn_ref[...], u_blk, (((1,), (1,)), ((), ())), preferred_element_type=jnp.float32)
    words = [w_ref[pl.ds(j * KEY_PAIRS_PER_STEP + c, tn, stride=HALF_KEYS), :] for c in range(KEY_PAIRS_PER_STEP)]
    wgt = jnp.concatenate([pltpu.bitcast(wd << 16, jnp.float32) for wd in words]
                          + [pltpu.bitcast(wd & jnp.int32(-65536), jnp.float32) for wd in words], axis=1)
    coef = (wgt * jax.nn.gelu(act)).astype(jnp.bfloat16)
    o_ref[...] += jnp.dot(coef, v_blk, preferred_element_type=jnp.float32)


def peer_eval(xn, k1, k2, gate, u, v, h):
    n, d = xn.shape
    tn = PEER_TILE
    eb = KEY_PAIRS_PER_STEP * N_KEYS
    n_exp = u.shape[0]
    tok = lambda i, j: (i, 0)
    halves = pl.BlockSpec((2, eb, d), lambda i, j: (0, j, 0))
    return pl.pallas_call(
        _peer_eval_kernel,
        grid=(n // tn, n_exp // (2 * eb)),
        in_specs=[pl.BlockSpec((tn, d), tok),
                  pl.BlockSpec((tn, N_ROUTES), tok), pl.BlockSpec((tn, N_ROUTES), tok),
                  pl.BlockSpec((tn, N_ROUTES), tok),
                  halves, halves,
                  pl.BlockSpec((tn, d), tok)],
        out_specs=pl.BlockSpec((tn, d), tok),
        out_shape=jax.ShapeDtypeStruct((n, d), jnp.float32),
        scratch_shapes=[pltpu.VMEM((tn * HALF_KEYS, N_KEYS), jnp.int32)],
        compiler_params=_compiler_params(PEER_VMEM_LIMIT_BYTES, dimension_semantics=("arbitrary", "arbitrary")),
        name="peer_eval",
    )(xn, k1, k2, gate, u.reshape(2, n_exp // 2, d), v.reshape(2, n_exp // 2, d), h)


def kernel(x_prompt, x_sample, cache_k, cache_v, cache_kidx, state_conv_a, state_conv_b, state_pool,
           page_table, meta_tokens, w_in, conv_a_w, conv_a_b, ln_a_g, ln_a_b, conv_b_w, pool_w,
           pool_scale, w_out, norm_mix_g, norm_ffn_g, peer_wq, peer_subkeys, peer_u, peer_v, norm_final_g):
    b, seq, d = x_prompt.shape
    db, dt, _ = x_sample.shape
    depth = w_in.shape[0]
    tp = N_META + seq
    n_p, n_s = b * tp, db * dt
    n_tok = n_p + n_s
    n_pad = pl.cdiv(n_tok, PEER_TILE) * PEER_TILE
    bf = jnp.bfloat16
    past_len = page_table.shape[1] * PAGE_SIZE

    meta = jnp.broadcast_to(meta_tokens[None], (b, N_META, d))
    hp = jnp.concatenate([meta, x_prompt], axis=1).reshape(n_p, d)
    h = jnp.concatenate([hp, x_sample.reshape(n_s, d), jnp.zeros((n_pad - n_tok, d), jnp.float32)], axis=0)

    news_p, news_s = [], []
    for l in range(depth):
        lw = {'conv_a_w': conv_a_w[l], 'conv_a_b': conv_a_b[l], 'ln_a_g': ln_a_g[l], 'ln_a_b': ln_a_b[l],
              'conv_b_w': conv_b_w[l], 'pool_w': pool_w[l], 'pool_scale': pool_scale[l]}
        past = {'cache_k': cache_k, 'cache_v': cache_v, 'cache_kidx': cache_kidx, 'layer': l,
                'conv_a': state_conv_a[l], 'conv_b': state_conv_b[l], 'pool': state_pool[l],
                'page_table': page_table}
        n_mix = 5 * GROUP_W
        w_proj = jnp.concatenate([w_in[l][:, :n_mix], w_in[l][:, N_COLS - GROUP_W:], w_in[l][:, n_mix:N_COLS - GROUP_W]],
                                 axis=1).astype(bf)
        z = norm_matmul(h, norm_mix_g[l], w_proj)
        ys_p, new_p = mixer_block(z[:n_p].reshape(b, tp, N_COLS), lw, None, 0)
        ys_s, new_s = mixer_block(z[n_p:n_tok].reshape(db, dt, N_COLS), lw, past, past_len)
        ys = [jnp.concatenate([yp.reshape(n_p, GROUP_W), ysm.reshape(n_s, GROUP_W),
                               jnp.zeros((n_pad - n_tok, GROUP_W), jnp.float32)], axis=0)
              for yp, ysm in zip(ys_p, ys_s)]
        h = out_proj(h, ys, w_out[l].astype(bf))
        xn, k1, k2, gate = peer_route(h, norm_ffn_g[l], peer_wq[l].T.astype(bf),
                                      peer_subkeys[l].reshape(PEER_HEADS * 2, N_KEYS, PEER_HALF).astype(bf))
        h = peer_eval(xn, k1, k2, gate, peer_u[l].astype(bf), peer_v[l].astype(bf), h)
        news_p.append(new_p)
        news_s.append(new_s)

    y = rms_norm_rows(h, norm_final_g)
    y_prompt = y[:n_p].reshape(b, tp, d)[:, N_META:]
    y_sample = y[n_p:n_tok].reshape(db, dt, d)
    outs_p = [jnp.stack([n[i] for n in news_p]) for i in range(6)]
    outs_s = [jnp.stack([n[i] for n in news_s]) for i in range(6)]
    return (y_prompt, y_sample, *outs_p, *outs_s)
```

```python
import functools

import jax
import jax.numpy as jnp
from jax import lax
from jax.experimental import pallas as pl
from jax.experimental.pallas import tpu as pltpu

D_MODEL = 1024
N_META = 16
GROUP_W = 256
CONV_A_WIDTH = 31
CONV_B_WIDTH = 3
N_HEADS = 4
HEAD_DIM = 64
N_IDX_HEADS = 8
IDX_DIM = 64
TOPK_MAX = 256
PAGE_SIZE = 128
POOL_WINDOWS = (2, 4, 8, 16)
POOL_GROUP = 64
POOL_CTX = 15
PEER_HEADS = 8
N_KEYS = 128
PEER_HALF = 128
PEER_TOPK = 16
N_ROUTES = PEER_HEADS * PEER_TOPK
RMS_EPS = 1e-6
LN_EPS = 1e-5
SPLIT_SIZES = (GROUP_W,) * 8 + (N_IDX_HEADS * IDX_DIM, IDX_DIM, N_IDX_HEADS, GROUP_W)
N_COLS = sum(SPLIT_SIZES)

Q_BLOCK = 128
KEY_CHUNK = 256
ROW_TILE = 256
ROUTE_TILE = 256
PEER_TILE = 512
HALF_KEYS = N_KEYS // 2
KEY_PAIRS_PER_STEP = 4
GATE_BUILD_UNROLL = 32
MIX_COLS = 6 * GROUP_W
MIX_TILE = 512
MIX_HALO = 32
SAMPLE_ROWS = 8
PAGES_PER_CHUNK = 4
SAMPLE_CHUNK = PAGES_PER_CHUNK * PAGE_SIZE
INT_MIN = -2 ** 31
NEG_INF_KEY = -2139095041
VMEM_LIMIT_BYTES = 48 * 1024 * 1024
PEER_VMEM_LIMIT_BYTES = 56 * 1024 * 1024


def _compiler_params(vmem_limit_bytes=VMEM_LIMIT_BYTES, **kw):
    return pltpu.CompilerParams(vmem_limit_bytes=vmem_limit_bytes, **kw)


def _norm_matmul_kernel(x_ref, g_ref, w_ref, o_ref):
    x = x_ref[...]
    y = x * lax.rsqrt(jnp.mean(x * x, axis=-1, keepdims=True) + RMS_EPS) * g_ref[...]
    o_ref[...] = jnp.dot(y.astype(jnp.bfloat16), w_ref[...], preferred_element_type=jnp.float32)


def norm_matmul(x, g, w):
    n, d = x.shape
    c = w.shape[1]
    return pl.pallas_call(
        _norm_matmul_kernel,
        grid=(n // ROW_TILE,),
        in_specs=[pl.BlockSpec((ROW_TILE, d), lambda i: (i, 0)),
                  pl.BlockSpec((1, d), lambda i: (0, 0)),
                  pl.BlockSpec((d, c), lambda i: (0, 0))],
        out_specs=pl.BlockSpec((ROW_TILE, c), lambda i: (i, 0)),
        out_shape=jax.ShapeDtypeStruct((n, c), jnp.float32),
        compiler_params=_compiler_params(),
        name="norm_matmul",
    )(x, g.reshape(1, d), w)


def _out_proj_kernel(h_ref, ya_ref, yb_ref, yc_ref, yd_ref, w_ref, o_ref):
    acc = h_ref[...]
    for gi, y_ref in enumerate((ya_ref, yb_ref, yc_ref, yd_ref)):
        acc += jnp.dot(y_ref[...].astype(jnp.bfloat16), w_ref[gi * GROUP_W:(gi + 1) * GROUP_W, :],
                       preferred_element_type=jnp.float32)
    o_ref[...] = acc


def out_proj(h, ys, w):
    n, d = h.shape
    y_spec = pl.BlockSpec((ROW_TILE, GROUP_W), lambda i: (i, 0))
    return pl.pallas_call(
        _out_proj_kernel,
        grid=(n // ROW_TILE,),
        in_specs=[pl.BlockSpec((ROW_TILE, d), lambda i: (i, 0)), y_spec, y_spec, y_spec, y_spec,
                  pl.BlockSpec(w.shape, lambda i: (0, 0))],
        out_specs=pl.BlockSpec((ROW_TILE, d), lambda i: (i, 0)),
        out_shape=jax.ShapeDtypeStruct((n, d), jnp.float32),
        compiler_params=_compiler_params(),
        name="out_proj",
    )(h, *ys, w)


def _rms_norm_kernel(x_ref, g_ref, o_ref):
    x = x_ref[...]
    o_ref[...] = x * lax.rsqrt(jnp.mean(x * x, axis=-1, keepdims=True) + RMS_EPS) * g_ref[...]


def rms_norm_rows(x, g):
    n, d = x.shape
    return pl.pallas_call(
        _rms_norm_kernel,
        grid=(n // ROW_TILE,),
        in_specs=[pl.BlockSpec((ROW_TILE, d), lambda i: (i, 0)), pl.BlockSpec((1, d), lambda i: (0, 0))],
        out_specs=pl.BlockSpec((ROW_TILE, d), lambda i: (i, 0)),
        out_shape=jax.ShapeDtypeStruct((n, d), jnp.float32),
        compiler_params=_compiler_params(),
        name="final_norm",
    )(x, g.reshape(1, d))


def _sortable_key(x):
    x = jnp.where(x == 0.0, 0.0, x)
    b = pltpu.bitcast(x, jnp.int32)
    return b ^ ((b >> 31) & 0x7FFFFFFF)


def _col_sum(x):
    return jnp.sum(x.reshape(x.shape[0] // 8, 8, x.shape[1]), axis=0)


def _dsa_prompt_kernel(qt_ref, qit_ref, wt_ref, kidx_ref, k_ref, vt_ref, o_ref, key_ref, *, topk):
    i = pl.program_id(1)
    n_chunks = (i + 2) // 2
    q_pos = i * Q_BLOCK + lax.broadcasted_iota(jnp.int32, (KEY_CHUNK, Q_BLOCK), 1)
    key_row = lax.broadcasted_iota(jnp.int32, (KEY_CHUNK, Q_BLOCK), 0)
    qit = qit_ref[...]
    wt = wt_ref[...]

    def score_chunk(c, carry):
        off = pl.multiple_of(c * KEY_CHUNK, KEY_CHUNK)
        s = jnp.dot(kidx_ref[pl.ds(off, KEY_CHUNK), :], qit, preferred_element_type=jnp.float32)
        s = jnp.maximum(s, 0.0)
        score = wt[0:1] * s[:, 0:Q_BLOCK]
        for h in range(1, N_IDX_HEADS):
            score = score + wt[h:h + 1] * s[:, h * Q_BLOCK:(h + 1) * Q_BLOCK]
        visible = (off + key_row) <= q_pos
        key_ref[c] = jnp.where(visible, _sortable_key(score), NEG_INF_KEY)
        return carry

    lax.fori_loop(0, n_chunks, score_chunk, 0)

    def count(pred):
        def body(c, acc):
            return acc + pred(key_ref[c]).astype(jnp.float32)
        acc = lax.fori_loop(0, n_chunks, body, jnp.zeros((KEY_CHUNK, Q_BLOCK), jnp.float32))
        return jnp.sum(acc, axis=0, keepdims=True)

    kf = float(topk)
    t0 = jnp.where(count(lambda k: k >= 0) >= kf, 0, INT_MIN).astype(jnp.int32)

    def search(it, t):
        cand = t + (jnp.int32(1) << (30 - it))
        return jnp.where(count(lambda k: k >= cand) >= kf, cand, t)

    thr = lax.fori_loop(0, 31, search, t0)
    n_take_eq = kf - count(lambda k: k > thr)

    d_head = lax.broadcasted_iota(jnp.int32, (GROUP_W, Q_BLOCK), 0) // HEAD_DIM
    qt = qt_ref[...]
    q_heads = jnp.concatenate([jnp.where(d_head == h, qt, jnp.zeros_like(qt)) for h in range(N_HEADS)], axis=1)
    tri = (lax.broadcasted_iota(jnp.int32, (KEY_CHUNK, KEY_CHUNK), 1)
           < lax.broadcasted_iota(jnp.int32, (KEY_CHUNK, KEY_CHUNK), 0)).astype(jnp.bfloat16)

    def attend_chunk(c, carry):
        eq_before, m, l, acc = carry
        off = pl.multiple_of(c * KEY_CHUNK, KEY_CHUNK)
        keys = key_ref[c]
        eq = keys == thr
        eq_excl = jnp.dot(tri, eq.astype(jnp.bfloat16), preferred_element_type=jnp.float32)
        sel = (keys > thr) | (eq & ((eq_before + eq_excl) < n_take_eq))
        sel = sel & (keys != NEG_INF_KEY)
        lg_all = jnp.dot(k_ref[pl.ds(off, KEY_CHUNK), :], q_heads,
                         preferred_element_type=jnp.float32) * (HEAD_DIM ** -0.5)
        vt = vt_ref[c]
        m_rows, l_rows, acc_rows = [], [], []
        for h in range(N_HEADS):
            lg = jnp.where(sel, lg_all[:, h * Q_BLOCK:(h + 1) * Q_BLOCK], -jnp.inf)
            m_old = m[h:h + 1]
            m_new = jnp.maximum(m_old, jnp.max(lg, axis=0, keepdims=True))
            m_safe = jnp.where(m_new == -jnp.inf, 0.0, m_new)
            p = jnp.exp(lg - m_safe)
            alpha = jnp.exp(m_old - m_safe)
            l_rows.append(alpha * l[h:h + 1] + jnp.sum(_col_sum(p), axis=0, keepdims=True))
            m_rows.append(m_new)
            pv = jnp.dot(vt[h * HEAD_DIM:(h + 1) * HEAD_DIM], p.astype(jnp.bfloat16),
                         preferred_element_type=jnp.float32)
            acc_rows.append(alpha * acc[h * HEAD_DIM:(h + 1) * HEAD_DIM] + pv)
        eq_after = eq_before + jnp.sum(_col_sum(eq.astype(jnp.float32)), axis=0, keepdims=True)
        return (eq_after, jnp.concatenate(m_rows, axis=0), jnp.concatenate(l_rows, axis=0),
                jnp.concatenate(acc_rows, axis=0))

    init = (jnp.zeros((1, Q_BLOCK), jnp.float32), jnp.full((N_HEADS, Q_BLOCK), -jnp.inf, jnp.float32),
            jnp.zeros((N_HEADS, Q_BLOCK), jnp.float32), jnp.zeros((GROUP_W, Q_BLOCK), jnp.float32))
    _, _, l, acc = lax.fori_loop(0, n_chunks, attend_chunk, init)
    out_t = jnp.concatenate([acc[h * HEAD_DIM:(h + 1) * HEAD_DIM] / l[h:h + 1] for h in range(N_HEADS)], axis=0)
    o_ref[...] = out_t.T


def dsa_prompt(q, k, v, q_idx, k_idx, w_idx, topk):
    b, t, _ = q.shape
    n_blk = pl.cdiv(t, Q_BLOCK)
    tq = n_blk * Q_BLOCK
    tk = pl.cdiv(tq, KEY_CHUNK) * KEY_CHUNK
    n_ch = tk // KEY_CHUNK
    bf = jnp.bfloat16
    pad_q = lambda a: jnp.pad(a, [(0, 0), (0, tq - t), (0, 0)])
    pad_k = lambda a: jnp.pad(a, [(0, 0), (0, tk - t), (0, 0)])
    qt = pad_q(q).astype(bf).reshape(b, n_blk, Q_BLOCK, GROUP_W).transpose(0, 1, 3, 2)
    qit = (pad_q(q_idx).astype(bf).reshape(b, n_blk, Q_BLOCK, N_IDX_HEADS, IDX_DIM)
           .transpose(0, 1, 4, 3, 2).reshape(b, n_blk, IDX_DIM, N_IDX_HEADS * Q_BLOCK))
    wt = pad_q(w_idx).reshape(b, n_blk, Q_BLOCK, N_IDX_HEADS).transpose(0, 1, 3, 2)
    vt = pad_k(v).astype(bf).reshape(b, n_ch, KEY_CHUNK, GROUP_W).transpose(0, 1, 3, 2)
    out = pl.pallas_call(
        functools.partial(_dsa_prompt_kernel, topk=topk),
        grid=(b, n_blk),
        in_specs=[
            pl.BlockSpec((None, None, GROUP_W, Q_BLOCK), lambda bi, i: (bi, i, 0, 0)),
            pl.BlockSpec((None, None, IDX_DIM, N_IDX_HEADS * Q_BLOCK), lambda bi, i: (bi, i, 0, 0)),
            pl.BlockSpec((None, None, N_IDX_HEADS, Q_BLOCK), lambda bi, i: (bi, i, 0, 0)),
            pl.BlockSpec((None, tk, IDX_DIM), lambda bi, i: (bi, 0, 0)),
            pl.BlockSpec((None, tk, GROUP_W), lambda bi, i: (bi, 0, 0)),
            pl.BlockSpec((None, n_ch, GROUP_W, KEY_CHUNK), lambda bi, i: (bi, 0, 0, 0)),
        ],
        out_specs=pl.BlockSpec((None, Q_BLOCK, GROUP_W), lambda bi, i: (bi, i, 0)),
        out_shape=jax.ShapeDtypeStruct((b, tq, GROUP_W), jnp.float32),
        scratch_shapes=[pltpu.VMEM((n_ch, KEY_CHUNK, Q_BLOCK), jnp.int32)],
        compiler_params=_compiler_params(),
        name="dsa_prompt",
    )(qt, qit, wt, pad_k(k_idx).astype(bf), pad_k(k).astype(bf), vt)
    return out[:, :t]


def _dsa_sample_kernel(pt_ref, q_ref, qi_ref, w_ref, kin_ref, kn_ref, vn_ref, kidx_hbm, k_hbm, v_hbm, o_ref,
                       kidx_buf, k_buf, v_buf, key_ref, m_ref, l_ref, acc_ref, sems, *, layer, n_pages, topk):
    b = pl.program_id(0)
    n_chunks = key_ref.shape[0]
    n_buf_pages = kidx_buf.shape[0]
    rows = SAMPLE_ROWS

    def page_copy(p, which):
        src, dst = ((kidx_hbm, kidx_buf), (k_hbm, k_buf), (v_hbm, v_buf))[which]
        return pltpu.make_async_copy(src.at[layer, pt_ref[b, p]], dst.at[p], sems.at[which])

    def start_page(p, carry):
        for which in range(3):
            page_copy(p, which).start()
        return carry

    lax.fori_loop(0, n_pages, start_page, 0)

    for p in range(n_pages, n_buf_pages):
        for buf, new_ref in ((kidx_buf, kin_ref), (k_buf, kn_ref), (v_buf, vn_ref)):
            buf[p] = new_ref[...].astype(buf.dtype) if p == n_pages else jnp.zeros(buf.shape[1:], buf.dtype)

    def wait_pages(which):
        def body(p, carry):
            page_copy(p, which).wait()
            return carry
        lax.fori_loop(0, n_pages, body, 0)

    wait_pages(0)

    past_len = n_pages * PAGE_SIZE
    q_row = lax.broadcasted_iota(jnp.int32, (rows, SAMPLE_CHUNK), 0)
    lane = lax.broadcasted_iota(jnp.int32, (rows, SAMPLE_CHUNK), 1)
    qi = qi_ref[...].reshape(N_IDX_HEADS * rows, IDX_DIM)
    w = w_ref[...]
    w_cols = [jnp.broadcast_to(w[:, h:h + 1], (rows, SAMPLE_CHUNK)) for h in range(N_IDX_HEADS)]

    def score_chunk(c, carry):
        p0 = pl.multiple_of(c * PAGES_PER_CHUNK, PAGES_PER_CHUNK)
        kc = kidx_buf[pl.ds(p0, PAGES_PER_CHUNK)].reshape(SAMPLE_CHUNK, IDX_DIM)
        s = lax.dot_general(qi, kc, (((1,), (1,)), ((), ())), preferred_element_type=jnp.float32)
        s = jnp.maximum(s, 0.0)
        score = w_cols[0] * s[0:rows]
        for h in range(1, N_IDX_HEADS):
            score = score + w_cols[h] * s[h * rows:(h + 1) * rows]
        visible = (c * SAMPLE_CHUNK + lane) <= (past_len + q_row)
        key_ref[c] = jnp.where(visible, _sortable_key(score), NEG_INF_KEY)
        return carry

    lax.fori_loop(0, n_chunks, score_chunk, 0)

    def count(pred):
        def body(c, acc):
            return acc + pred(key_ref[c]).astype(jnp.float32)
        acc = lax.fori_loop(0, n_chunks, body, jnp.zeros((rows, SAMPLE_CHUNK), jnp.float32))
        return jnp.sum(acc, axis=-1, keepdims=True)

    kf = float(topk)
    t0 = jnp.where(count(lambda k: k >= 0) >= kf, 0, INT_MIN).astype(jnp.int32)

    def search(it, t):
        cand = t + (jnp.int32(1) << (30 - it))
        return jnp.where(count(lambda k: k >= cand) >= kf, cand, t)

    thr = lax.fori_loop(0, 31, search, t0)
    n_take_eq = kf - count(lambda k: k > thr)

    wait_pages(1)
    wait_pages(2)

    head_of_lane = lax.broadcasted_iota(jnp.int32, (rows, GROUP_W), 1) // HEAD_DIM
    q = q_ref[...]
    q_heads = jnp.concatenate([jnp.where(head_of_lane == h, q, jnp.zeros_like(q)) for h in range(N_HEADS)], axis=0)
    tri = (lax.broadcasted_iota(jnp.int32, (SAMPLE_CHUNK, SAMPLE_CHUNK), 0)
           < lax.broadcasted_iota(jnp.int32, (SAMPLE_CHUNK, SAMPLE_CHUNK), 1)).astype(jnp.bfloat16)

    m_ref[...] = jnp.full(m_ref.shape, -jnp.inf, jnp.float32)
    l_ref[...] = jnp.zeros(l_ref.shape, jnp.float32)
    acc_ref[...] = jnp.zeros(acc_ref.shape, jnp.float32)

    def attend_chunk(c, eq_before):
        p0 = pl.multiple_of(c * PAGES_PER_CHUNK, PAGES_PER_CHUNK)
        keys = key_ref[c]
        eq = keys == thr
        eq_excl = jnp.dot(eq.astype(jnp.bfloat16), tri, preferred_element_type=jnp.float32)
        sel = (keys > thr) | (eq & ((eq_before + eq_excl) < n_take_eq))
        sel = sel & (keys != NEG_INF_KEY)
        sel_all = jnp.concatenate([sel] * N_HEADS, axis=0)
        kc = k_buf[pl.ds(p0, PAGES_PER_CHUNK)].reshape(SAMPLE_CHUNK, GROUP_W)
        vc = v_buf[pl.ds(p0, PAGES_PER_CHUNK)].reshape(SAMPLE_CHUNK, GROUP_W)
        lg = lax.dot_general(q_heads, kc, (((1,), (1,)), ((), ())),
                             preferred_element_type=jnp.float32) * (HEAD_DIM ** -0.5)
        lg = jnp.where(sel_all, lg, -jnp.inf)
        m_old = m_ref[...]
        m_new = jnp.maximum(m_old, jnp.max(lg, axis=-1, keepdims=True))
        m_safe = jnp.where(m_new == -jnp.inf, 0.0, m_new)
        p = jnp.exp(lg - m_safe)
        alpha = jnp.exp(m_old - m_safe)
        l_ref[...] = alpha * l_ref[...] + jnp.sum(p, axis=-1, keepdims=True)
        m_ref[...] = m_new
        acc_ref[...] = alpha * acc_ref[...] + jnp.dot(p.astype(jnp.bfloat16), vc, preferred_element_type=jnp.float32)
        return eq_before + jnp.sum(eq.astype(jnp.float32), axis=-1, keepdims=True)

    lax.fori_loop(0, n_chunks, attend_chunk, jnp.zeros((rows, 1), jnp.float32))

    res = acc_ref[...] / l_ref[...]
    out = jnp.zeros((rows, GROUP_W), jnp.float32)
    for h in range(N_HEADS):
        out = jnp.where(head_of_lane == h, res[h * rows:(h + 1) * rows], out)
    o_ref[...] = out


def dsa_sample(q, k, v, q_idx, k_idx, w_idx, cache_k, cache_v, cache_kidx, page_table, layer):
    db, t, _ = q.shape
    n_pages = page_table.shape[1]
    depth, n_pool = cache_k.shape[:2]
    topk = min(TOPK_MAX, (n_pages * PAGE_SIZE + t) // 4)
    rows = SAMPLE_ROWS
    n_chunks = pl.cdiv(n_pages + 1, PAGES_PER_CHUNK)
    n_buf_pages = n_chunks * PAGES_PER_CHUNK
    bf = jnp.bfloat16
    pad_rows = lambda a, r: jnp.pad(a, [(0, 0), (0, r - t), (0, 0)])
    qi = pad_rows(q_idx, rows).astype(bf).reshape(db, rows, N_IDX_HEADS, IDX_DIM).transpose(0, 2, 1, 3)
    per_b = lambda *shape: pl.BlockSpec((None,) + shape, lambda b, pt: (b,) + (0,) * len(shape))
    hbm = pl.BlockSpec(memory_space=pl.ANY)
    grid_spec = pltpu.PrefetchScalarGridSpec(
        num_scalar_prefetch=1,
        grid=(db,),
        in_specs=[per_b(rows, GROUP_W), per_b(N_IDX_HEADS, rows, IDX_DIM), per_b(rows, N_IDX_HEADS),
                  per_b(PAGE_SIZE, IDX_DIM), per_b(PAGE_SIZE, GROUP_W), per_b(PAGE_SIZE, GROUP_W),
                  hbm, hbm, hbm],
        out_specs=per_b(rows, GROUP_W),
        scratch_shapes=[
            pltpu.VMEM((n_buf_pages, PAGE_SIZE, IDX_DIM), bf),
            pltpu.VMEM((n_buf_pages, PAGE_SIZE, GROUP_W), bf),
            pltpu.VMEM((n_buf_pages, PAGE_SIZE, GROUP_W), bf),
            pltpu.VMEM((n_chunks, rows, SAMPLE_CHUNK), jnp.int32),
            pltpu.VMEM((N_HEADS * rows, 1), jnp.float32),
            pltpu.VMEM((N_HEADS * rows, 1), jnp.float32),
            pltpu.VMEM((N_HEADS * rows, GROUP_W), jnp.float32),
            pltpu.SemaphoreType.DMA((3,)),
        ],
    )
    out = pl.pallas_call(
        functools.partial(_dsa_sample_kernel, layer=layer, n_pages=n_pages, topk=topk),
        grid_spec=grid_spec,
        out_shape=jax.ShapeDtypeStruct((db, rows, GROUP_W), jnp.float32),
        compiler_params=_compiler_params(),
        name="dsa_sample",
    )(page_table, pad_rows(q, rows).astype(bf), qi, pad_rows(w_idx, rows),
      pad_rows(k_idx, PAGE_SIZE), pad_rows(k, PAGE_SIZE), pad_rows(v, PAGE_SIZE),
      cache_kidx.astype(bf), cache_k.astype(bf).reshape(depth, n_pool, PAGE_SIZE, GROUP_W),
      cache_v.astype(bf).reshape(depth, n_pool, PAGE_SIZE, GROUP_W))
    return out[:, :t]


def _mixer_streams(z):
    a_val, a_gate, b_gate, c_gate, b_in, pool_in = [z[:, g * GROUP_W:(g + 1) * GROUP_W] for g in range(6)]
    return a_val * jax.nn.sigmoid(a_gate), c_gate * b_in, pool_in, b_gate


def _mixers_kernel(z_ref, ctx_ref, wa_ref, ba_ref, lng_ref, lnb_ref, wb_ref, wp_ref, ps_ref,
                   ya_ref, yb_ref, yd_ref, ta_ref, tb_ref, tp_ref, ext_ref, *, ctx_from_z, start_pos, n_valid_last):
    i = pl.program_id(1)
    tt = z_ref.shape[0]
    u, g_in, p_in, b_gate = _mixer_streams(z_ref[...])
    if ctx_from_z:
        hu, hg, hp, _ = _mixer_streams(ctx_ref[...])
        hu, hg, hp = [jnp.where(i > 0, a, 0.0) for a in (hu, hg, hp)]
    else:
        hu, hg, hp = [ctx_ref[:, g * GROUP_W:(g + 1) * GROUP_W] for g in range(3)]
    for s, (head, body) in enumerate(((hu, u), (hg, g_in), (hp, p_in))):
        ext_ref[s, 0:MIX_HALO] = head
        ext_ref[s, MIX_HALO:MIX_HALO + tt] = body

    def rows(s, back):
        return ext_ref[s, pl.ds(MIX_HALO - back, tt), :]

    acc = jnp.zeros((tt, GROUP_W), jnp.float32)
    for j in range(CONV_A_WIDTH):
        acc = acc + wa_ref[j:j + 1, :] * rows(0, CONV_A_WIDTH - 1 - j)
    acc = acc + ba_ref[...]
    mu = jnp.mean(acc, axis=-1, keepdims=True)
    var = jnp.mean(jnp.square(acc - mu), axis=-1, keepdims=True)
    ln = (acc - mu) * lax.rsqrt(var + LN_EPS) * lng_ref[...] + lnb_ref[...]
    ya_ref[...] = ln * jax.nn.sigmoid(ln)

    cb = jnp.zeros((tt, GROUP_W), jnp.float32)
    for j in range(CONV_B_WIDTH):
        cb = cb + wb_ref[j:j + 1, :] * rows(1, CONV_B_WIDTH - 1 - j)
    yb_ref[...] = b_gate * cb

    lane_group = lax.broadcasted_iota(jnp.int32, (tt, GROUP_W), 1) // POOL_GROUP
    pos1 = start_pos + i * tt + lax.broadcasted_iota(jnp.int32, (tt, GROUP_W), 0) + 1
    run = p_in
    mean = jnp.zeros((tt, GROUP_W), jnp.float32)
    back = 1
    for gi, w in enumerate(POOL_WINDOWS):
        while back < w:
            run = run + rows(2, back)
            back += 1
        cnt = jnp.minimum(pos1, w).astype(jnp.float32)
        mean = jnp.where(lane_group == gi, run / cnt, mean)
    pooled = (mean - p_in).astype(jnp.bfloat16)
    yd_ref[...] = jnp.dot(pooled, wp_ref[...], preferred_element_type=jnp.float32) * ps_ref[...]

    @pl.when(i == pl.num_programs(1) - 1)
    def _():
        end = MIX_HALO + n_valid_last
        ta_ref[...] = ext_ref[0, end - (CONV_A_WIDTH - 1):end, :]
        tb_ref[...] = ext_ref[1, end - (CONV_B_WIDTH - 1):end, :]
        tp_ref[...] = ext_ref[2, end - POOL_CTX:end, :]


def mixers(z, ctx, lw, start_pos):
    b, t, _ = z.shape
    tt = min(MIX_TILE, t)
    n_tiles = pl.cdiv(t, tt)
    ctx_from_z = ctx is None
    if ctx_from_z:
        per_tile = tt // MIX_HALO
        ctx_arr = z
        ctx_spec = pl.BlockSpec((None, MIX_HALO, MIX_COLS), lambda bi, i: (bi, jnp.maximum(i * per_tile - 1, 0), 0))
    else:
        ctx_arr = ctx
        ctx_spec = pl.BlockSpec((None, MIX_HALO, 3 * GROUP_W), lambda bi, i: (bi, 0, 0))
    wp = jnp.zeros((GROUP_W, GROUP_W), jnp.float32)
    for gi in range(len(POOL_WINDOWS)):
        wp = wp.at[gi * POOL_GROUP:(gi + 1) * POOL_GROUP, gi * POOL_GROUP:(gi + 1) * POOL_GROUP].set(lw['pool_w'][gi])
    row = lambda a: a.reshape(1, GROUP_W)
    const = lambda shape: pl.BlockSpec(shape, lambda bi, i: (0,) * len(shape))
    y_spec = pl.BlockSpec((None, tt, GROUP_W), lambda bi, i: (bi, i, 0))
    tail = lambda r: pl.BlockSpec((None, r, GROUP_W), lambda bi, i: (bi, 0, 0))
    f32 = jnp.float32
    return pl.pallas_call(
        functools.partial(_mixers_kernel, ctx_from_z=ctx_from_z, start_pos=start_pos,
                          n_valid_last=t - (n_tiles - 1) * tt),
        grid=(b, n_tiles),
        in_specs=[pl.BlockSpec((None, tt, MIX_COLS), lambda bi, i: (bi, i, 0)), ctx_spec,
                  const((CONV_A_WIDTH, GROUP_W)), const((1, GROUP_W)), const((1, GROUP_W)), const((1, GROUP_W)),
                  const((CONV_B_WIDTH, GROUP_W)), const((GROUP_W, GROUP_W)), const((1, GROUP_W))],
        out_specs=[y_spec, y_spec, y_spec, tail(CONV_A_WIDTH - 1), tail(CONV_B_WIDTH - 1), tail(POOL_CTX)],
        out_shape=[jax.ShapeDtypeStruct((b, t, GROUP_W), f32)] * 3
                  + [jax.ShapeDtypeStruct((b, r, GROUP_W), f32) for r in (CONV_A_WIDTH - 1, CONV_B_WIDTH - 1, POOL_CTX)],
        scratch_shapes=[pltpu.VMEM((3, MIX_HALO + tt, GROUP_W), f32)],
        compiler_params=_compiler_params(),
        name="mixers",
    )(z, ctx_arr, lw['conv_a_w'], row(lw['conv_a_b']), row(lw['ln_a_g']), row(lw['ln_a_b']),
      lw['conv_b_w'], wp.astype(jnp.bfloat16), row(lw['pool_scale']))


def mixer_block(z, lw, past, start_pos):
    b, t, _ = z.shape
    parts, start = [], MIX_COLS
    for size in (GROUP_W, GROUP_W, GROUP_W, N_IDX_HEADS * IDX_DIM, IDX_DIM, N_IDX_HEADS):
        parts.append(z[..., start:start + size])
        start += size
    q, k, v, q_idx, k_idx, w_idx = parts
    if past is None:
        ctx = None
        y_c = dsa_prompt(q, k, v, q_idx, k_idx, w_idx, min(TOPK_MAX, t // 4))
    else:
        front = lambda a: jnp.pad(a, [(0, 0), (MIX_HALO - a.shape[1], 0), (0, 0)])
        ctx = jnp.concatenate([front(past['conv_a']), front(past['conv_b']), front(past['pool'])], axis=-1)
        y_c = dsa_sample(q, k, v, q_idx, k_idx, w_idx, past['cache_k'], past['cache_v'], past['cache_kidx'],
                         past['page_table'], past['layer'])
    y_a, y_b, y_d, tail_a, tail_b, tail_p = mixers(z, ctx, lw, start_pos)
    new = (k.reshape(b, t, N_HEADS, HEAD_DIM), v.reshape(b, t, N_HEADS, HEAD_DIM), k_idx, tail_a, tail_b, tail_p)
    return (y_a, y_b, y_c, y_d), new


def _top_rows(s, k):
    r = s.shape[0]
    row = lax.broadcasted_iota(jnp.int32, s.shape, 0)
    vals, idxs = [], []
    for _ in range(k):
        m = jnp.max(s, axis=0, keepdims=True)
        idx = jnp.min(jnp.where(s == m, row, r), axis=0, keepdims=True)
        s = jnp.where(row == idx, -jnp.inf, s)
        vals.append(m)
        idxs.append(idx)
    return jnp.concatenate(vals, axis=0), jnp.concatenate(idxs, axis=0)


def _pick_rows(table, which):
    row = lax.broadcasted_iota(jnp.int32, table.shape, 0)
    out = []
    for j in range(which.shape[0]):
        out.append(jnp.sum(jnp.where(row == which[j:j + 1], table, 0), axis=0, keepdims=True))
    return jnp.concatenate(out, axis=0)


def _cand_a(r):
    return jnp.where(r < 16, 0, jnp.where(r < 72, ((r - 16) >> 3) + 1, r - 64))


def _cand_b(r):
    return jnp.where(r < 16, r, jnp.where(r < 72, (r - 16) & 7, 0))


def _peer_route_kernel(h_ref, g_ref, wqt_ref, sk_ref, xn_ref, k1_ref, k2_ref, gate_ref,
                       qt_ref, k1t_ref, k2t_ref, gt_ref):
    x = h_ref[...]
    xn = (x * lax.rsqrt(jnp.mean(x * x, axis=-1, keepdims=True) + RMS_EPS) * g_ref[...]).astype(jnp.bfloat16)
    xn_ref[...] = xn
    qt_ref[...] = lax.dot_general(wqt_ref[...], xn, (((1,), (1,)), ((), ())),
                                  preferred_element_type=jnp.float32).astype(jnp.bfloat16)

    def head(h, carry):
        r0 = pl.multiple_of(h * 2 * PEER_HALF, 2 * PEER_HALF)
        s1 = jnp.dot(sk_ref[2 * h], qt_ref[pl.ds(r0, PEER_HALF), :], preferred_element_type=jnp.float32)
        s2 = jnp.dot(sk_ref[2 * h + 1], qt_ref[pl.ds(r0 + PEER_HALF, PEER_HALF), :],
                     preferred_element_type=jnp.float32)
        v1, i1 = _top_rows(s1, PEER_TOPK)
        v2, i2 = _top_rows(s2, PEER_TOPK)
        cand = jnp.concatenate([v1[0:1] + v2] + [v1[a:a + 1] + v2[0:8] for a in range(1, 8)] + [v1[8:16] + v2[0:1]],
                               axis=0)
        r = lax.broadcasted_iota(jnp.int32, cand.shape, 0)
        cand = jnp.where((_cand_a(r) + 1) * (_cand_b(r) + 1) <= PEER_TOPK, cand, -jnp.inf)
        best, ridx = _top_rows(cand, PEER_TOPK)
        e = jnp.exp(best - best[0:1])
        o0 = pl.multiple_of(h * PEER_TOPK, PEER_TOPK)
        k1t_ref[pl.ds(o0, PEER_TOPK), :] = _pick_rows(i1, _cand_a(ridx))
        k2t_ref[pl.ds(o0, PEER_TOPK), :] = _pick_rows(i2, _cand_b(ridx))
        gt_ref[pl.ds(o0, PEER_TOPK), :] = e / jnp.sum(e, axis=0, keepdims=True)
        return carry

    lax.fori_loop(0, PEER_HEADS, head, 0)
    k1_ref[...] = k1t_ref[...].T
    k2_ref[...] = k2t_ref[...].T
    gate_ref[...] = gt_ref[...].T


def peer_route(h, g, wqt, sk):
    n, d = h.shape
    tm = ROUTE_TILE
    row_spec = pl.BlockSpec((tm, N_ROUTES), lambda i: (i, 0))
    return pl.pallas_call(
        _peer_route_kernel,
        grid=(n // tm,),
        in_specs=[pl.BlockSpec((tm, d), lambda i: (i, 0)),
                  pl.BlockSpec((1, d), lambda i: (0, 0)),
                  pl.BlockSpec(wqt.shape, lambda i: (0, 0)),
                  pl.BlockSpec(sk.shape, lambda i: (0, 0, 0))],
        out_specs=[pl.BlockSpec((tm, d), lambda i: (i, 0)), row_spec, row_spec, row_spec],
        out_shape=[jax.ShapeDtypeStruct((n, d), jnp.bfloat16),
                   jax.ShapeDtypeStruct((n, N_ROUTES), jnp.int32),
                   jax.ShapeDtypeStruct((n, N_ROUTES), jnp.int32),
                   jax.ShapeDtypeStruct((n, N_ROUTES), jnp.float32)],
        scratch_shapes=[pltpu.VMEM((PEER_HEADS * 2 * PEER_HALF, tm), jnp.bfloat16),
                        pltpu.VMEM((N_ROUTES, tm), jnp.int32),
                        pltpu.VMEM((N_ROUTES, tm), jnp.int32),
                        pltpu.VMEM((N_ROUTES, tm), jnp.float32)],
        compiler_params=_compiler_params(),
        name="peer_route",
    )(h, g.reshape(1, d), wqt, sk)


def _peer_eval_kernel(xn_ref, k1_ref, k2_ref, gate_ref, u_ref, v_ref, h_ref, o_ref, w_ref):
    j = pl.program_id(1)
    tn = xn_ref.shape[0]

    @pl.when(j == 0)
    def _():
        sub = lax.broadcasted_iota(jnp.int32, (N_KEYS, N_ROUTES), 0)

        def token(n, carry):
            k1 = jnp.broadcast_to(k1_ref[pl.ds(n, 1), :], (N_KEYS, N_ROUTES))
            k2 = jnp.broadcast_to(k2_ref[pl.ds(n, 1), :], (N_KEYS, N_ROUTES))
            gt = jnp.broadcast_to(gate_ref[pl.ds(n, 1), :], (N_KEYS, N_ROUTES))
            left = jnp.where(sub == k1, gt, 0.0).astype(jnp.bfloat16)
            right = (sub == k2).astype(jnp.bfloat16)
            tile = lax.dot_general(left, right, (((1,), (1,)), ((), ())), preferred_element_type=jnp.float32)
            tile = tile.astype(jnp.bfloat16).astype(jnp.float32)
            lo = lax.shift_right_logical(pltpu.bitcast(tile[:HALF_KEYS], jnp.int32), 16)
            hi = pltpu.bitcast(tile[HALF_KEYS:], jnp.int32) & jnp.int32(-65536)
            w_ref[pl.ds(pl.multiple_of(n * HALF_KEYS, HALF_KEYS), HALF_KEYS), :] = hi | lo
            return carry

        lax.fori_loop(0, tn, token, 0, unroll=GATE_BUILD_UNROLL)
        o_ref[...] = h_ref[...]

    eb = KEY_PAIRS_PER_STEP * N_KEYS
    u_blk = u_ref[...].reshape(2 * eb, u_ref.shape[-1])
    v_blk = v_ref[...].reshape(2 * eb, v_ref.shape[-1])
    act = lax.dot_general(xn_ref[...], u_blk, (((1,), (1,)), ((), ())), preferred_element_type=jnp.float32)
    words = [w_ref[pl.ds(j * KEY_PAIRS_PER_STEP + c, tn, stride=HALF_KEYS), :] for c in range(KEY_PAIRS_PER_STEP)]
    wgt = jnp.concatenate([pltpu.bitcast(wd << 16, jnp.float32) for wd in words]
                          + [pltpu.bitcast(wd & jnp.int32(-65536), jnp.float32) for wd in words], axis=1)
    coef = (wgt * jax.nn.gelu(act)).astype(jnp.bfloat16)
    o_ref[...] += jnp.dot(coef, v_blk, preferred_element_type=jnp.float32)


def peer_eval(xn, k1, k2, gate, u, v, h):
    n, d = xn.shape
    tn = PEER_TILE
    eb = KEY_PAIRS_PER_STEP * N_KEYS
    n_exp = u.shape[0]
    tok = lambda i, j: (i, 0)
    halves = pl.BlockSpec((2, eb, d), lambda i, j: (0, j, 0))
    return pl.pallas_call(
        _peer_eval_kernel,
        grid=(n // tn, n_exp // (2 * eb)),
        in_specs=[pl.BlockSpec((tn, d), tok),
                  pl.BlockSpec((tn, N_ROUTES), tok), pl.BlockSpec((tn, N_ROUTES), tok),
                  pl.BlockSpec((tn, N_ROUTES), tok),
                  halves, halves,
                  pl.BlockSpec((tn, d), tok)],
        out_specs=pl.BlockSpec((tn, d), tok),
        out_shape=jax.ShapeDtypeStruct((n, d), jnp.float32),
        scratch_shapes=[pltpu.VMEM((tn * HALF_KEYS, N_KEYS), jnp.int32)],
        compiler_params=_compiler_params(PEER_VMEM_LIMIT_BYTES, dimension_semantics=("arbitrary", "arbitrary")),
        name="peer_eval",
    )(xn, k1, k2, gate, u.reshape(2, n_exp // 2, d), v.reshape(2, n_exp // 2, d), h)


def kernel(x_prompt, x_sample, cache_k, cache_v, cache_kidx, state_conv_a, state_conv_b, state_pool,
           page_table, meta_tokens, w_in, conv_a_w, conv_a_b, ln_a_g, ln_a_b, conv_b_w, pool_w,
           pool_scale, w_out, norm_mix_g, norm_ffn_g, peer_wq, peer_subkeys, peer_u, peer_v, norm_final_g):
    b, seq, d = x_prompt.shape
    db, dt, _ = x_sample.shape
    depth = w_in.shape[0]
    tp = N_META + seq
    n_p, n_s = b * tp, db * dt
    n_tok = n_p + n_s
    n_pad = pl.cdiv(n_tok, PEER_TILE) * PEER_TILE
    bf = jnp.bfloat16
    past_len = page_table.shape[1] * PAGE_SIZE

    meta = jnp.broadcast_to(meta_tokens[None], (b, N_META, d))
    hp = jnp.concatenate([meta, x_prompt], axis=1).reshape(n_p, d)
    h = jnp.concatenate([hp, x_sample.reshape(n_s, d), jnp.zeros((n_pad - n_tok, d), jnp.float32)], axis=0)

    news_p, news_s = [], []
    for l in range(depth):
        lw = {'conv_a_w': conv_a_w[l], 'conv_a_b': conv_a_b[l], 'ln_a_g': ln_a_g[l], 'ln_a_b': ln_a_b[l],
              'conv_b_w': conv_b_w[l], 'pool_w': pool_w[l], 'pool_scale': pool_scale[l]}
        past = {'cache_k': cache_k, 'cache_v': cache_v, 'cache_kidx': cache_kidx, 'layer': l,
                'conv_a': state_conv_a[l], 'conv_b': state_conv_b[l], 'pool': state_pool[l],
                'page_table': page_table}
        n_mix = 5 * GROUP_W
        w_proj = jnp.concatenate([w_in[l][:, :n_mix], w_in[l][:, N_COLS - GROUP_W:], w_in[l][:, n_mix:N_COLS - GROUP_W]],
                                 axis=1).astype(bf)
        z = norm_matmul(h, norm_mix_g[l], w_proj)
        ys_p, new_p = mixer_block(z[:n_p].reshape(b, tp, N_COLS), lw, None, 0)
        ys_s, new_s = mixer_block(z[n_p:n_tok].reshape(db, dt, N_COLS), lw, past, past_len)
        ys = [jnp.concatenate([yp.reshape(n_p, GROUP_W), ysm.reshape(n_s, GROUP_W),
                               jnp.zeros((n_pad - n_tok, GROUP_W), jnp.float32)], axis=0)
              for yp, ysm in zip(ys_p, ys_s)]
        h = out_proj(h, ys, w_out[l].astype(bf))
        xn, k1, k2, gate = peer_route(h, norm_ffn_g[l], peer_wq[l].T.astype(bf),
                                      peer_subkeys[l].reshape(PEER_HEADS * 2, N_KEYS, PEER_HALF).astype(bf))
        h = peer_eval(xn, k1, k2, gate, peer_u[l].astype(bf), peer_v[l].astype(bf), h)
        news_p.append(new_p)
        news_s.append(new_s)

    y = rms_norm_rows(h, norm_final_g)
    y_prompt = y[:n_p].reshape(b, tp, d)[:, N_META:]
    y_sample = y[n_p:n_tok].reshape(db, dt, d)
    outs_p = [jnp.stack([n[i] for n in news_p]) for i in range(6)]
    outs_s = [jnp.stack([n[i] for n in news_s]) for i in range(6)]
    return (y_prompt, y_sample, *outs_p, *outs_s)
```

```python
import functools

import jax
import jax.numpy as jnp
from jax import lax
from jax.experimental import pallas as pl
from jax.experimental.pallas import tpu as pltpu

D_MODEL = 1024
N_META = 16
GROUP_W = 256
CONV_A_WIDTH = 31
CONV_B_WIDTH = 3
N_HEADS = 4
HEAD_DIM = 64
N_IDX_HEADS = 8
IDX_DIM = 64
TOPK_MAX = 256
PAGE_SIZE = 128
POOL_WINDOWS = (2, 4, 8, 16)
POOL_GROUP = 64
POOL_CTX = 15
PEER_HEADS = 8
N_KEYS = 128
PEER_HALF = 128
PEER_TOPK = 16
N_ROUTES = PEER_HEADS * PEER_TOPK
RMS_EPS = 1e-6
LN_EPS = 1e-5
SPLIT_SIZES = (GROUP_W,) * 8 + (N_IDX_HEADS * IDX_DIM, IDX_DIM, N_IDX_HEADS, GROUP_W)
N_COLS = sum(SPLIT_SIZES)

Q_BLOCK = 128
KEY_CHUNK = 512
ROW_TILE = 256
ROUTE_TILE = 256
PEER_TILE = 512
HALF_KEYS = N_KEYS // 2
KEY_PAIRS_PER_STEP = 4
GATE_BUILD_UNROLL = 32
MIX_COLS = 6 * GROUP_W
MIX_TILE = 512
MIX_HALO = 32
SAMPLE_ROWS = 8
SAMPLE_GROUP = 2
PAGES_PER_CHUNK = 4
SAMPLE_CHUNK = PAGES_PER_CHUNK * PAGE_SIZE
INT_MIN = -2 ** 31
NEG_INF_KEY = -2139095041
VMEM_LIMIT_BYTES = 48 * 1024 * 1024
PEER_VMEM_LIMIT_BYTES = 56 * 1024 * 1024


def _compiler_params(vmem_limit_bytes=VMEM_LIMIT_BYTES, **kw):
    return pltpu.CompilerParams(vmem_limit_bytes=vmem_limit_bytes, **kw)


def _norm_matmul_kernel(x_ref, g_ref, w_ref, o_ref):
    x = x_ref[...]
    y = x * lax.rsqrt(jnp.mean(x * x, axis=-1, keepdims=True) + RMS_EPS) * g_ref[...]
    o_ref[...] = jnp.dot(y.astype(jnp.bfloat16), w_ref[...], preferred_element_type=jnp.float32)


def norm_matmul(x, g, w):
    n, d = x.shape
    c = w.shape[1]
    return pl.pallas_call(
        _norm_matmul_kernel,
        grid=(n // ROW_TILE,),
        in_specs=[pl.BlockSpec((ROW_TILE, d), lambda i: (i, 0)),
                  pl.BlockSpec((1, d), lambda i: (0, 0)),
                  pl.BlockSpec((d, c), lambda i: (0, 0))],
        out_specs=pl.BlockSpec((ROW_TILE, c), lambda i: (i, 0)),
        out_shape=jax.ShapeDtypeStruct((n, c), jnp.float32),
        compiler_params=_compiler_params(),
        name="norm_matmul",
    )(x, g.reshape(1, d), w)


def _out_proj_kernel(h_ref, ya_ref, yb_ref, yc_ref, yd_ref, w_ref, o_ref):
    acc = h_ref[...]
    for gi, y_ref in enumerate((ya_ref, yb_ref, yc_ref, yd_ref)):
        acc += jnp.dot(y_ref[...].astype(jnp.bfloat16), w_ref[gi * GROUP_W:(gi + 1) * GROUP_W, :],
                       preferred_element_type=jnp.float32)
    o_ref[...] = acc


def out_proj(h, ys, w):
    n, d = h.shape
    y_spec = pl.BlockSpec((ROW_TILE, GROUP_W), lambda i: (i, 0))
    return pl.pallas_call(
        _out_proj_kernel,
        grid=(n // ROW_TILE,),
        in_specs=[pl.BlockSpec((ROW_TILE, d), lambda i: (i, 0)), y_spec, y_spec, y_spec, y_spec,
                  pl.BlockSpec(w.shape, lambda i: (0, 0))],
        out_specs=pl.BlockSpec((ROW_TILE, d), lambda i: (i, 0)),
        out_shape=jax.ShapeDtypeStruct((n, d), jnp.float32),
        compiler_params=_compiler_params(),
        name="out_proj",
    )(h, *ys, w)


def _rms_norm_kernel(x_ref, g_ref, o_ref):
    x = x_ref[...]
    o_ref[...] = x * lax.rsqrt(jnp.mean(x * x, axis=-1, keepdims=True) + RMS_EPS) * g_ref[...]


def rms_norm_rows(x, g):
    n, d = x.shape
    return pl.pallas_call(
        _rms_norm_kernel,
        grid=(n // ROW_TILE,),
        in_specs=[pl.BlockSpec((ROW_TILE, d), lambda i: (i, 0)), pl.BlockSpec((1, d), lambda i: (0, 0))],
        out_specs=pl.BlockSpec((ROW_TILE, d), lambda i: (i, 0)),
        out_shape=jax.ShapeDtypeStruct((n, d), jnp.float32),
        compiler_params=_compiler_params(),
        name="final_norm",
    )(x, g.reshape(1, d))


def _sortable_key(x):
    x = jnp.where(x == 0.0, 0.0, x)
    b = pltpu.bitcast(x, jnp.int32)
    return b ^ ((b >> 31) & 0x7FFFFFFF)


def _col_sum(x):
    return jnp.sum(x.reshape(x.shape[0] // 8, 8, x.shape[1]), axis=0)


def _dsa_prompt_kernel(qt_ref, qit_ref, wt_ref, kidx_ref, k_ref, vt_ref, o_ref, key_ref, *, topk):
    i = pl.program_id(1)
    n_chunks = pl.cdiv((i + 1) * Q_BLOCK, KEY_CHUNK)
    q_pos = i * Q_BLOCK + lax.broadcasted_iota(jnp.int32, (KEY_CHUNK, Q_BLOCK), 1)
    key_row = lax.broadcasted_iota(jnp.int32, (KEY_CHUNK, Q_BLOCK), 0)
    qit = qit_ref[...]
    wt = wt_ref[...]

    def score_chunk(c, carry):
        off = pl.multiple_of(c * KEY_CHUNK, KEY_CHUNK)
        s = jnp.dot(kidx_ref[pl.ds(off, KEY_CHUNK), :], qit, preferred_element_type=jnp.float32)
        s = jnp.maximum(s, 0.0)
        score = wt[0:1] * s[:, 0:Q_BLOCK]
        for h in range(1, N_IDX_HEADS):
            score = score + wt[h:h + 1] * s[:, h * Q_BLOCK:(h + 1) * Q_BLOCK]
        visible = (off + key_row) <= q_pos
        key_ref[c] = jnp.where(visible, _sortable_key(score), NEG_INF_KEY)
        return carry

    lax.fori_loop(0, n_chunks, score_chunk, 0)

    def count(pred):
        def body(c, acc):
            p = pred(key_ref[c]).astype(jnp.float32)
            return acc + p[:KEY_CHUNK // 2] + p[KEY_CHUNK // 2:]
        acc = lax.fori_loop(0, n_chunks, body, jnp.zeros((KEY_CHUNK // 2, Q_BLOCK), jnp.float32))
        return jnp.sum(acc, axis=0, keepdims=True)

    kf = float(topk)
    t0 = jnp.where(count(lambda k: k >= 0) >= kf, 0, INT_MIN).astype(jnp.int32)

    def search(it, t):
        cand = t + (jnp.int32(1) << (30 - it))
        return jnp.where(count(lambda k: k >= cand) >= kf, cand, t)

    thr = lax.fori_loop(0, 31, search, t0)
    n_take_eq = kf - count(lambda k: k > thr)

    d_head = lax.broadcasted_iota(jnp.int32, (GROUP_W, Q_BLOCK), 0) // HEAD_DIM
    qt = qt_ref[...]
    q_heads = jnp.concatenate([jnp.where(d_head == h, qt, jnp.zeros_like(qt)) for h in range(N_HEADS)], axis=1)
    tri = (lax.broadcasted_iota(jnp.int32, (KEY_CHUNK, KEY_CHUNK), 1)
           < lax.broadcasted_iota(jnp.int32, (KEY_CHUNK, KEY_CHUNK), 0)).astype(jnp.bfloat16)

    def attend_chunk(c, carry):
        eq_before, m, l, acc = carry
        off = pl.multiple_of(c * KEY_CHUNK, KEY_CHUNK)
        keys = key_ref[c]
        eq = keys == thr
        eq_excl = jnp.dot(tri, eq.astype(jnp.bfloat16), preferred_element_type=jnp.float32)
        sel = (keys > thr) | (eq & ((eq_before + eq_excl) < n_take_eq))
        sel = sel & (keys != NEG_INF_KEY)
        lg_all = jnp.dot(k_ref[pl.ds(off, KEY_CHUNK), :], q_heads,
                         preferred_element_type=jnp.float32) * (HEAD_DIM ** -0.5)
        vt = vt_ref[c]
        m_rows, l_rows, acc_rows = [], [], []
        for h in range(N_HEADS):
            lg = jnp.where(sel, lg_all[:, h * Q_BLOCK:(h + 1) * Q_BLOCK], -jnp.inf)
            m_old = m[h:h + 1]
            m_new = jnp.maximum(m_old, jnp.max(lg, axis=0, keepdims=True))
            m_safe = jnp.where(m_new == -jnp.inf, 0.0, m_new)
            p = jnp.exp(lg - m_safe)
            alpha = jnp.exp(m_old - m_safe)
            l_rows.append(alpha * l[h:h + 1] + jnp.sum(_col_sum(p), axis=0, keepdims=True))
            m_rows.append(m_new)
            pv = jnp.dot(vt[h * HEAD_DIM:(h + 1) * HEAD_DIM], p.astype(jnp.bfloat16),
                         preferred_element_type=jnp.float32)
            acc_rows.append(alpha * acc[h * HEAD_DIM:(h + 1) * HEAD_DIM] + pv)
        eq_after = eq_before + jnp.sum(_col_sum(eq.astype(jnp.float32)), axis=0, keepdims=True)
        return (eq_after, jnp.concatenate(m_rows, axis=0), jnp.concatenate(l_rows, axis=0),
                jnp.concatenate(acc_rows, axis=0))

    init = (jnp.zeros((1, Q_BLOCK), jnp.float32), jnp.full((N_HEADS, Q_BLOCK), -jnp.inf, jnp.float32),
            jnp.zeros((N_HEADS, Q_BLOCK), jnp.float32), jnp.zeros((GROUP_W, Q_BLOCK), jnp.float32))
    _, _, l, acc = lax.fori_loop(0, n_chunks, attend_chunk, init)
    out_t = jnp.concatenate([acc[h * HEAD_DIM:(h + 1) * HEAD_DIM] / l[h:h + 1] for h in range(N_HEADS)], axis=0)
    o_ref[...] = out_t.T


def dsa_prompt(q, k, v, q_idx, k_idx, w_idx, topk):
    b, t, _ = q.shape
    n_blk = pl.cdiv(t, Q_BLOCK)
    tq = n_blk * Q_BLOCK
    tk = pl.cdiv(tq, KEY_CHUNK) * KEY_CHUNK
    n_ch = tk // KEY_CHUNK
    bf = jnp.bfloat16
    pad_q = lambda a: jnp.pad(a, [(0, 0), (0, tq - t), (0, 0)])
    pad_k = lambda a: jnp.pad(a, [(0, 0), (0, tk - t), (0, 0)])
    qt = pad_q(q).astype(bf).reshape(b, n_blk, Q_BLOCK, GROUP_W).transpose(0, 1, 3, 2)
    qit = (pad_q(q_idx).astype(bf).reshape(b, n_blk, Q_BLOCK, N_IDX_HEADS, IDX_DIM)
           .transpose(0, 1, 4, 3, 2).reshape(b, n_blk, IDX_DIM, N_IDX_HEADS * Q_BLOCK))
    wt = pad_q(w_idx).reshape(b, n_blk, Q_BLOCK, N_IDX_HEADS).transpose(0, 1, 3, 2)
    vt = pad_k(v).astype(bf).reshape(b, n_ch, KEY_CHUNK, GROUP_W).transpose(0, 1, 3, 2)
    out = pl.pallas_call(
        functools.partial(_dsa_prompt_kernel, topk=topk),
        grid=(b, n_blk),
        in_specs=[
            pl.BlockSpec((None, None, GROUP_W, Q_BLOCK), lambda bi, i: (bi, i, 0, 0)),
            pl.BlockSpec((None, None, IDX_DIM, N_IDX_HEADS * Q_BLOCK), lambda bi, i: (bi, i, 0, 0)),
            pl.BlockSpec((None, None, N_IDX_HEADS, Q_BLOCK), lambda bi, i: (bi, i, 0, 0)),
            pl.BlockSpec((None, tk, IDX_DIM), lambda bi, i: (bi, 0, 0)),
            pl.BlockSpec((None, tk, GROUP_W), lambda bi, i: (bi, 0, 0)),
            pl.BlockSpec((None, n_ch, GROUP_W, KEY_CHUNK), lambda bi, i: (bi, 0, 0, 0)),
        ],
        out_specs=pl.BlockSpec((None, Q_BLOCK, GROUP_W), lambda bi, i: (bi, i, 0)),
        out_shape=jax.ShapeDtypeStruct((b, tq, GROUP_W), jnp.float32),
        scratch_shapes=[pltpu.VMEM((n_ch, KEY_CHUNK, Q_BLOCK), jnp.int32)],
        compiler_params=_compiler_params(),
        name="dsa_prompt",
    )(qt, qit, wt, pad_k(k_idx).astype(bf), pad_k(k).astype(bf), vt)
    return out[:, :t]


def _dsa_sample_kernel(pt_ref, q_ref, qi_ref, w_ref, kin_ref, kn_ref, vn_ref, kidx_hbm, k_hbm, v_hbm, o_ref,
                       kidx_buf, k_buf, v_buf, key_ref, m_ref, l_ref, acc_ref, sems, *, layer, n_pages, topk):
    step = pl.program_id(0)
    n_chunks = key_ref.shape[0]
    n_buf_pages = kidx_buf.shape[1]
    rows = SAMPLE_ROWS
    group = SAMPLE_GROUP

    def page_copy(g, p, which):
        src, dst = ((kidx_hbm, kidx_buf), (k_hbm, k_buf), (v_hbm, v_buf))[which]
        return pltpu.make_async_copy(src.at[layer, pt_ref[step * group + g, p]], dst.at[g, p], sems.at[which])

    def start_page(p, carry):
        for g in range(group):
            for which in range(3):
                page_copy(g, p, which).start()
        return carry

    lax.fori_loop(0, n_pages, start_page, 0)

    for g in range(group):
        for p in range(n_pages, n_buf_pages):
            for buf, new_ref in ((kidx_buf, kin_ref), (k_buf, kn_ref), (v_buf, vn_ref)):
                buf[g, p] = new_ref[g].astype(buf.dtype) if p == n_pages else jnp.zeros(buf.shape[2:], buf.dtype)

    def wait_pages(which):
        def body(p, carry):
            for g in range(group):
                page_copy(g, p, which).wait()
            return carry
        lax.fori_loop(0, n_pages, body, 0)

    wait_pages(0)

    past_len = n_pages * PAGE_SIZE
    q_row = lax.broadcasted_iota(jnp.int32, (group * rows, SAMPLE_CHUNK), 0) % rows
    lane = lax.broadcasted_iota(jnp.int32, (group * rows, SAMPLE_CHUNK), 1)
    qi = [qi_ref[g].reshape(N_IDX_HEADS * rows, IDX_DIM) for g in range(group)]
    w_cols = [[jnp.broadcast_to(w_ref[g][:, h:h + 1], (rows, SAMPLE_CHUNK)) for h in range(N_IDX_HEADS)]
              for g in range(group)]

    def score_chunk(c, carry):
        p0 = pl.multiple_of(c * PAGES_PER_CHUNK, PAGES_PER_CHUNK)
        scores = []
        for g in range(group):
            kc = kidx_buf[g, pl.ds(p0, PAGES_PER_CHUNK)].reshape(SAMPLE_CHUNK, IDX_DIM)
            s = lax.dot_general(qi[g], kc, (((1,), (1,)), ((), ())), preferred_element_type=jnp.float32)
            s = jnp.maximum(s, 0.0)
            score = w_cols[g][0] * s[0:rows]
            for h in range(1, N_IDX_HEADS):
                score = score + w_cols[g][h] * s[h * rows:(h + 1) * rows]
            scores.append(score)
        visible = (c * SAMPLE_CHUNK + lane) <= (past_len + q_row)
        key_ref[c] = jnp.where(visible, _sortable_key(jnp.concatenate(scores, axis=0)), NEG_INF_KEY)
        return carry

    lax.fori_loop(0, n_chunks, score_chunk, 0)

    def count(pred):
        def body(c, acc):
            return acc + pred(key_ref[c]).astype(jnp.float32)
        acc = lax.fori_loop(0, n_chunks, body, jnp.zeros((group * rows, SAMPLE_CHUNK), jnp.float32))
        return jnp.sum(acc, axis=-1, keepdims=True)

    kf = float(topk)
    t0 = jnp.where(count(lambda k: k >= 0) >= kf, 0, INT_MIN).astype(jnp.int32)

    def search(it, t):
        cand = t + (jnp.int32(1) << (30 - it))
        return jnp.where(count(lambda k: k >= cand) >= kf, cand, t)

    thr = lax.fori_loop(0, 31, search, t0)
    n_take_eq = kf - count(lambda k: k > thr)

    wait_pages(1)
    wait_pages(2)

    head_of_lane = lax.broadcasted_iota(jnp.int32, (rows, GROUP_W), 1) // HEAD_DIM
    q_heads = [jnp.concatenate([jnp.where(head_of_lane == h, q_ref[g], jnp.zeros((rows, GROUP_W), q_ref.dtype))
                                for h in range(N_HEADS)], axis=0) for g in range(group)]
    tri = (lax.broadcasted_iota(jnp.int32, (SAMPLE_CHUNK, SAMPLE_CHUNK), 0)
           < lax.broadcasted_iota(jnp.int32, (SAMPLE_CHUNK, SAMPLE_CHUNK), 1)).astype(jnp.bfloat16)

    m_ref[...] = jnp.full(m_ref.shape, -jnp.inf, jnp.float32)
    l_ref[...] = jnp.zeros(l_ref.shape, jnp.float32)
    acc_ref[...] = jnp.zeros(acc_ref.shape, jnp.float32)
    hr = N_HEADS * rows

    def attend_chunk(c, eq_before):
        p0 = pl.multiple_of(c * PAGES_PER_CHUNK, PAGES_PER_CHUNK)
        keys = key_ref[c]
        eq = keys == thr
        eq_excl = jnp.dot(eq.astype(jnp.bfloat16), tri, preferred_element_type=jnp.float32)
        sel = (keys > thr) | (eq & ((eq_before + eq_excl) < n_take_eq))
        sel = sel & (keys != NEG_INF_KEY)
        lgs, vcs = [], []
        for g in range(group):
            kc = k_buf[g, pl.ds(p0, PAGES_PER_CHUNK)].reshape(SAMPLE_CHUNK, GROUP_W)
            vcs.append(v_buf[g, pl.ds(p0, PAGES_PER_CHUNK)].reshape(SAMPLE_CHUNK, GROUP_W))
            lg = lax.dot_general(q_heads[g], kc, (((1,), (1,)), ((), ())),
                                 preferred_element_type=jnp.float32) * (HEAD_DIM ** -0.5)
            sel_g = jnp.concatenate([sel[g * rows:(g + 1) * rows]] * N_HEADS, axis=0)
            lgs.append(jnp.where(sel_g, lg, -jnp.inf))
        lg = jnp.concatenate(lgs, axis=0)
        m_old = m_ref[...]
        m_new = jnp.maximum(m_old, jnp.max(lg, axis=-1, keepdims=True))
        m_safe = jnp.where(m_new == -jnp.inf, 0.0, m_new)
        p = jnp.exp(lg - m_safe)
        alpha = jnp.exp(m_old - m_safe)
        l_ref[...] = alpha * l_ref[...] + jnp.sum(p, axis=-1, keepdims=True)
        m_ref[...] = m_new
        pb = p.astype(jnp.bfloat16)
        pv = jnp.concatenate([jnp.dot(pb[g * hr:(g + 1) * hr], vcs[g], preferred_element_type=jnp.float32)
                              for g in range(group)], axis=0)
        acc_ref[...] = alpha * acc_ref[...] + pv
        return eq_before + jnp.sum(eq.astype(jnp.float32), axis=-1, keepdims=True)

    lax.fori_loop(0, n_chunks, attend_chunk, jnp.zeros((group * rows, 1), jnp.float32))

    res = acc_ref[...] / l_ref[...]
    for g in range(group):
        out = jnp.zeros((rows, GROUP_W), jnp.float32)
        for h in range(N_HEADS):
            out = jnp.where(head_of_lane == h, res[g * hr + h * rows:g * hr + (h + 1) * rows], out)
        o_ref[g] = out


def dsa_sample(q, k, v, q_idx, k_idx, w_idx, cache_k, cache_v, cache_kidx, page_table, layer):
    db, t, _ = q.shape
    n_pages = page_table.shape[1]
    depth, n_pool = cache_k.shape[:2]
    topk = min(TOPK_MAX, (n_pages * PAGE_SIZE + t) // 4)
    rows = SAMPLE_ROWS
    group = SAMPLE_GROUP
    assert db % group == 0
    n_chunks = pl.cdiv(n_pages + 1, PAGES_PER_CHUNK)
    n_buf_pages = n_chunks * PAGES_PER_CHUNK
    bf = jnp.bfloat16
    pad_rows = lambda a, r: jnp.pad(a, [(0, 0), (0, r - t), (0, 0)])
    qi = pad_rows(q_idx, rows).astype(bf).reshape(db, rows, N_IDX_HEADS, IDX_DIM).transpose(0, 2, 1, 3)
    per_g = lambda *shape: pl.BlockSpec((group,) + shape, lambda s, pt: (s,) + (0,) * len(shape))
    hbm = pl.BlockSpec(memory_space=pl.ANY)
    grid_spec = pltpu.PrefetchScalarGridSpec(
        num_scalar_prefetch=1,
        grid=(db // group,),
        in_specs=[per_g(rows, GROUP_W), per_g(N_IDX_HEADS, rows, IDX_DIM), per_g(rows, N_IDX_HEADS),
                  per_g(PAGE_SIZE, IDX_DIM), per_g(PAGE_SIZE, GROUP_W), per_g(PAGE_SIZE, GROUP_W),
                  hbm, hbm, hbm],
        out_specs=per_g(rows, GROUP_W),
        scratch_shapes=[
            pltpu.VMEM((group, n_buf_pages, PAGE_SIZE, IDX_DIM), bf),
            pltpu.VMEM((group, n_buf_pages, PAGE_SIZE, GROUP_W), bf),
            pltpu.VMEM((group, n_buf_pages, PAGE_SIZE, GROUP_W), bf),
            pltpu.VMEM((n_chunks, group * rows, SAMPLE_CHUNK), jnp.int32),
            pltpu.VMEM((group * N_HEADS * rows, 1), jnp.float32),
            pltpu.VMEM((group * N_HEADS * rows, 1), jnp.float32),
            pltpu.VMEM((group * N_HEADS * rows, GROUP_W), jnp.float32),
            pltpu.SemaphoreType.DMA((3,)),
        ],
    )
    out = pl.pallas_call(
        functools.partial(_dsa_sample_kernel, layer=layer, n_pages=n_pages, topk=topk),
        grid_spec=grid_spec,
        out_shape=jax.ShapeDtypeStruct((db, rows, GROUP_W), jnp.float32),
        compiler_params=_compiler_params(),
        name="dsa_sample",
    )(page_table, pad_rows(q, rows).astype(bf), qi, pad_rows(w_idx, rows),
      pad_rows(k_idx, PAGE_SIZE), pad_rows(k, PAGE_SIZE), pad_rows(v, PAGE_SIZE),
      cache_kidx.astype(bf), cache_k.astype(bf).reshape(depth, n_pool, PAGE_SIZE, GROUP_W),
      cache_v.astype(bf).reshape(depth, n_pool, PAGE_SIZE, GROUP_W))
    return out[:, :t]


def _mixer_streams(z):
    a_val, a_gate, b_gate, c_gate, b_in, pool_in = [z[:, g * GROUP_W:(g + 1) * GROUP_W] for g in range(6)]
    return a_val * jax.nn.sigmoid(a_gate), c_gate * b_in, pool_in, b_gate


def _mixers_kernel(z_ref, ctx_ref, wa_ref, ba_ref, lng_ref, lnb_ref, wb_ref, wp_ref, ps_ref,
                   ya_ref, yb_ref, yd_ref, ta_ref, tb_ref, tp_ref, ext_ref, *, ctx_from_z, start_pos, n_valid_last):
    i = pl.program_id(1)
    tt = z_ref.shape[0]
    u, g_in, p_in, b_gate = _mixer_streams(z_ref[...])
    if ctx_from_z:
        hu, hg, hp, _ = _mixer_streams(ctx_ref[...])
        hu, hg, hp = [jnp.where(i > 0, a, 0.0) for a in (hu, hg, hp)]
    else:
        hu, hg, hp = [ctx_ref[:, g * GROUP_W:(g + 1) * GROUP_W] for g in range(3)]
    for s, (head, body) in enumerate(((hu, u), (hg, g_in), (hp, p_in))):
        ext_ref[s, 0:MIX_HALO] = head
        ext_ref[s, MIX_HALO:MIX_HALO + tt] = body

    def rows(s, back):
        return ext_ref[s, pl.ds(MIX_HALO - back, tt), :]

    acc = jnp.zeros((tt, GROUP_W), jnp.float32)
    for j in range(CONV_A_WIDTH):
        acc = acc + wa_ref[j:j + 1, :] * rows(0, CONV_A_WIDTH - 1 - j)
    acc = acc + ba_ref[...]
    mu = jnp.mean(acc, axis=-1, keepdims=True)
    var = jnp.mean(jnp.square(acc - mu), axis=-1, keepdims=True)
    ln = (acc - mu) * lax.rsqrt(var + LN_EPS) * lng_ref[...] + lnb_ref[...]
    ya_ref[...] = ln * jax.nn.sigmoid(ln)

    cb = jnp.zeros((tt, GROUP_W), jnp.float32)
    for j in range(CONV_B_WIDTH):
        cb = cb + wb_ref[j:j + 1, :] * rows(1, CONV_B_WIDTH - 1 - j)
    yb_ref[...] = b_gate * cb

    lane_group = lax.broadcasted_iota(jnp.int32, (tt, GROUP_W), 1) // POOL_GROUP
    pos1 = start_pos + i * tt + lax.broadcasted_iota(jnp.int32, (tt, GROUP_W), 0) + 1
    run = p_in
    mean = jnp.zeros((tt, GROUP_W), jnp.float32)
    back = 1
    for gi, w in enumerate(POOL_WINDOWS):
        while back < w:
            run = run + rows(2, back)
            back += 1
        cnt = jnp.minimum(pos1, w).astype(jnp.float32)
        mean = jnp.where(lane_group == gi, run / cnt, mean)
    pooled = (mean - p_in).astype(jnp.bfloat16)
    yd_ref[...] = jnp.dot(pooled, wp_ref[...], preferred_element_type=jnp.float32) * ps_ref[...]

    @pl.when(i == pl.num_programs(1) - 1)
    def _():
        end = MIX_HALO + n_valid_last
        ta_ref[...] = ext_ref[0, end - (CONV_A_WIDTH - 1):end, :]
        tb_ref[...] = ext_ref[1, end - (CONV_B_WIDTH - 1):end, :]
        tp_ref[...] = ext_ref[2, end - POOL_CTX:end, :]


def mixers(z, ctx, lw, start_pos):
    b, t, _ = z.shape
    tt = min(MIX_TILE, t)
    n_tiles = pl.cdiv(t, tt)
    ctx_from_z = ctx is None
    if ctx_from_z:
        per_tile = tt // MIX_HALO
        ctx_arr = z
        ctx_spec = pl.BlockSpec((None, MIX_HALO, MIX_COLS), lambda bi, i: (bi, jnp.maximum(i * per_tile - 1, 0), 0))
    else:
        ctx_arr = ctx
        ctx_spec = pl.BlockSpec((None, MIX_HALO, 3 * GROUP_W), lambda bi, i: (bi, 0, 0))
    wp = jnp.zeros((GROUP_W, GROUP_W), jnp.float32)
    for gi in range(len(POOL_WINDOWS)):
        wp = wp.at[gi * POOL_GROUP:(gi + 1) * POOL_GROUP, gi * POOL_GROUP:(gi + 1) * POOL_GROUP].set(lw['pool_w'][gi])
    row = lambda a: a.reshape(1, GROUP_W)
    const = lambda shape: pl.BlockSpec(shape, lambda bi, i: (0,) * len(shape))
    y_spec = pl.BlockSpec((None, tt, GROUP_W), lambda bi, i: (bi, i, 0))
    tail = lambda r: pl.BlockSpec((None, r, GROUP_W), lambda bi, i: (bi, 0, 0))
    f32 = jnp.float32
    return pl.pallas_call(
        functools.partial(_mixers_kernel, ctx_from_z=ctx_from_z, start_pos=start_pos,
                          n_valid_last=t - (n_tiles - 1) * tt),
        grid=(b, n_tiles),
        in_specs=[pl.BlockSpec((None, tt, MIX_COLS), lambda bi, i: (bi, i, 0)), ctx_spec,
                  const((CONV_A_WIDTH, GROUP_W)), const((1, GROUP_W)), const((1, GROUP_W)), const((1, GROUP_W)),
                  const((CONV_B_WIDTH, GROUP_W)), const((GROUP_W, GROUP_W)), const((1, GROUP_W))],
        out_specs=[y_spec, y_spec, y_spec, tail(CONV_A_WIDTH - 1), tail(CONV_B_WIDTH - 1), tail(POOL_CTX)],
        out_shape=[jax.ShapeDtypeStruct((b, t, GROUP_W), f32)] * 3
                  + [jax.ShapeDtypeStruct((b, r, GROUP_W), f32) for r in (CONV_A_WIDTH - 1, CONV_B_WIDTH - 1, POOL_CTX)],
        scratch_shapes=[pltpu.VMEM((3, MIX_HALO + tt, GROUP_W), f32)],
        compiler_params=_compiler_params(),
        name="mixers",
    )(z, ctx_arr, lw['conv_a_w'], row(lw['conv_a_b']), row(lw['ln_a_g']), row(lw['ln_a_b']),
      lw['conv_b_w'], wp.astype(jnp.bfloat16), row(lw['pool_scale']))


def mixer_block(z, lw, past, start_pos):
    b, t, _ = z.shape
    parts, start = [], MIX_COLS
    for size in (GROUP_W, GROUP_W, GROUP_W, N_IDX_HEADS * IDX_DIM, IDX_DIM, N_IDX_HEADS):
        parts.append(z[..., start:start + size])
        start += size
    q, k, v, q_idx, k_idx, w_idx = parts
    if past is None:
        ctx = None
        y_c = dsa_prompt(q, k, v, q_idx, k_idx, w_idx, min(TOPK_MAX, t // 4))
    else:
        front = lambda a: jnp.pad(a, [(0, 0), (MIX_HALO - a.shape[1], 0), (0, 0)])
        ctx = jnp.concatenate([front(past['conv_a']), front(past['conv_b']), front(past['pool'])], axis=-1)
        y_c = dsa_sample(q, k, v, q_idx, k_idx, w_idx, past['cache_k'], past['cache_v'], past['cache_kidx'],
                         past['page_table'], past['layer'])
    y_a, y_b, y_d, tail_a, tail_b, tail_p = mixers(z, ctx, lw, start_pos)
    new = (k.reshape(b, t, N_HEADS, HEAD_DIM), v.reshape(b, t, N_HEADS, HEAD_DIM), k_idx, tail_a, tail_b, tail_p)
    return (y_a, y_b, y_c, y_d), new


def _top_rows(s, k):
    r = s.shape[0]
    row = lax.broadcasted_iota(jnp.int32, s.shape, 0)
    vals, idxs = [], []
    for _ in range(k):
        m = jnp.max(s, axis=0, keepdims=True)
        idx = jnp.min(jnp.where(s == m, row, r), axis=0, keepdims=True)
        s = jnp.where(row == idx, -jnp.inf, s)
        vals.append(m)
        idxs.append(idx)
    return jnp.concatenate(vals, axis=0), jnp.concatenate(idxs, axis=0)


def _pick_rows(table, which):
    row = lax.broadcasted_iota(jnp.int32, table.shape, 0)
    out = []
    for j in range(which.shape[0]):
        out.append(jnp.sum(jnp.where(row == which[j:j + 1], table, 0), axis=0, keepdims=True))
    return jnp.concatenate(out, axis=0)


def _cand_a(r):
    return jnp.where(r < 16, 0, jnp.where(r < 72, ((r - 16) >> 3) + 1, r - 64))


def _cand_b(r):
    return jnp.where(r < 16, r, jnp.where(r < 72, (r - 16) & 7, 0))


def _peer_route_kernel(h_ref, g_ref, wqt_ref, sk_ref, xn_ref, k1_ref, k2_ref, gate_ref,
                       qt_ref, k1t_ref, k2t_ref, gt_ref):
    x = h_ref[...]
    xn = (x * lax.rsqrt(jnp.mean(x * x, axis=-1, keepdims=True) + RMS_EPS) * g_ref[...]).astype(jnp.bfloat16)
    xn_ref[...] = xn
    qt_ref[...] = lax.dot_general(wqt_ref[...], xn, (((1,), (1,)), ((), ())),
                                  preferred_element_type=jnp.float32).astype(jnp.bfloat16)

    def head(h, carry):
        r0 = pl.multiple_of(h * 2 * PEER_HALF, 2 * PEER_HALF)
        s1 = jnp.dot(sk_ref[2 * h], qt_ref[pl.ds(r0, PEER_HALF), :], preferred_element_type=jnp.float32)
        s2 = jnp.dot(sk_ref[2 * h + 1], qt_ref[pl.ds(r0 + PEER_HALF, PEER_HALF), :],
                     preferred_element_type=jnp.float32)
        v1, i1 = _top_rows(s1, PEER_TOPK)
        v2, i2 = _top_rows(s2, PEER_TOPK)
        cand = jnp.concatenate([v1[0:1] + v2] + [v1[a:a + 1] + v2[0:8] for a in range(1, 8)] + [v1[8:16] + v2[0:1]],
                               axis=0)
        r = lax.broadcasted_iota(jnp.int32, cand.shape, 0)
        cand = jnp.where((_cand_a(r) + 1) * (_cand_b(r) + 1) <= PEER_TOPK, cand, -jnp.inf)
        best, ridx = _top_rows(cand, PEER_TOPK)
        e = jnp.exp(best - best[0:1])
        o0 = pl.multiple_of(h * PEER_TOPK, PEER_TOPK)
        k1t_ref[pl.ds(o0, PEER_TOPK), :] = _pick_rows(i1, _cand_a(ridx))
        k2t_ref[pl.ds(o0, PEER_TOPK), :] = _pick_rows(i2, _cand_b(ridx))
        gt_ref[pl.ds(o0, PEER_TOPK), :] = e / jnp.sum(e, axis=0, keepdims=True)
        return carry

    lax.fori_loop(0, PEER_HEADS, head, 0)
    k1_ref[...] = k1t_ref[...].T
    k2_ref[...] = k2t_ref[...].T
    gate_ref[...] = gt_ref[...].T


def peer_route(h, g, wqt, sk):
    n, d = h.shape
    tm = ROUTE_TILE
    row_spec = pl.BlockSpec((tm, N_ROUTES), lambda i: (i, 0))
    return pl.pallas_call(
        _peer_route_kernel,
        grid=(n // tm,),
        in_specs=[pl.BlockSpec((tm, d), lambda i: (i, 0)),
                  pl.BlockSpec((1, d), lambda i: (0, 0)),
                  pl.BlockSpec(wqt.shape, lambda i: (0, 0)),
                  pl.BlockSpec(sk.shape, lambda i: (0, 0, 0))],
        out_specs=[pl.BlockSpec((tm, d), lambda i: (i, 0)), row_spec, row_spec, row_spec],
        out_shape=[jax.ShapeDtypeStruct((n, d), jnp.bfloat16),
                   jax.ShapeDtypeStruct((n, N_ROUTES), jnp.int32),
                   jax.ShapeDtypeStruct((n, N_ROUTES), jnp.int32),
                   jax.ShapeDtypeStruct((n, N_ROUTES), jnp.float32)],
        scratch_shapes=[pltpu.VMEM((PEER_HEADS * 2 * PEER_HALF, tm), jnp.bfloat16),
                        pltpu.VMEM((N_ROUTES, tm), jnp.int32),
                        pltpu.VMEM((N_ROUTES, tm), jnp.int32),
                        pltpu.VMEM((N_ROUTES, tm), jnp.float32)],
        compiler_params=_compiler_params(),
        name="peer_route",
    )(h, g.reshape(1, d), wqt, sk)


def _peer_eval_kernel(xn_ref, k1_ref, k2_ref, gate_ref, u_ref, v_ref, h_ref, o_ref, w_ref):
    j = pl.program_id(1)
    tn = xn_ref.shape[0]

    @pl.when(j == 0)
    def _():
        sub = lax.broadcasted_iota(jnp.int32, (N_KEYS, N_ROUTES), 0)

        def token(n, carry):
            k1 = jnp.broadcast_to(k1_ref[pl.ds(n, 1), :], (N_KEYS, N_ROUTES))
            k2 = jnp.broadcast_to(k2_ref[pl.ds(n, 1), :], (N_KEYS, N_ROUTES))
            gt = jnp.broadcast_to(gate_ref[pl.ds(n, 1), :], (N_KEYS, N_ROUTES))
            left = jnp.where(sub == k1, gt, 0.0).astype(jnp.bfloat16)
            right = (sub == k2).astype(jnp.bfloat16)
            tile = lax.dot_general(left, right, (((1,), (1,)), ((), ())), preferred_element_type=jnp.float32)
            tile = tile.astype(jnp.bfloat16).astype(jnp.float32)
            lo = lax.shift_right_logical(pltpu.bitcast(tile[:HALF_KEYS], jnp.int32), 16)
            hi = pltpu.bitcast(tile[HALF_KEYS:], jnp.int32) & jnp.int32(-65536)
            w_ref[pl.ds(pl.multiple_of(n * HALF_KEYS, HALF_KEYS), HALF_KEYS), :] = hi | lo
            return carry

        lax.fori_loop(0, tn, token, 0, unroll=GATE_BUILD_UNROLL)
        o_ref[...] = h_ref[...]

    eb = KEY_PAIRS_PER_STEP * N_KEYS
    u_blk = u_ref[...].reshape(2 * eb, u_ref.shape[-1])
    v_blk = v_ref[...].reshape(2 * eb, v_ref.shape[-1])
    act = lax.dot_general(xn_ref[...], u_blk, (((1,), (1,)), ((), ())), preferred_element_type=jnp.float32)
    words = [w_ref[pl.ds(j * KEY_PAIRS_PER_STEP + c, tn, stride=HALF_KEYS), :] for c in range(KEY_PAIRS_PER_STEP)]
    wgt = jnp.concatenate([pltpu.bitcast(wd << 16, jnp.float32) for wd in words]
                          + [pltpu.bitcast(wd & jnp.int32(-65536), jnp.float32) for wd in words], axis=1)
    coef = (wgt * jax.nn.gelu(act)).astype(jnp.bfloat16)
    o_ref[...] += jnp.dot(coef, v_blk, preferred_element_type=jnp.float32)


def peer_eval(xn, k1, k2, gate, u, v, h):
    n, d = xn.shape
    tn = PEER_TILE
    eb = KEY_PAIRS_PER_STEP * N_KEYS
    n_exp = u.shape[0]
    tok = lambda i, j: (i, 0)
    halves = pl.BlockSpec((2, eb, d), lambda i, j: (0, j, 0))
    return pl.pallas_call(
        _peer_eval_kernel,
        grid=(n // tn, n_exp // (2 * eb)),
        in_specs=[pl.BlockSpec((tn, d), tok),
                  pl.BlockSpec((tn, N_ROUTES), tok), pl.BlockSpec((tn, N_ROUTES), tok),
                  pl.BlockSpec((tn, N_ROUTES), tok),
                  halves, halves,
                  pl.BlockSpec((tn, d), tok)],
        out_specs=pl.BlockSpec((tn, d), tok),
        out_shape=jax.ShapeDtypeStruct((n, d), jnp.float32),
        scratch_shapes=[pltpu.VMEM((tn * HALF_KEYS, N_KEYS), jnp.int32)],
        compiler_params=_compiler_params(PEER_VMEM_LIMIT_BYTES, dimension_semantics=("arbitrary", "arbitrary")),
        name="peer_eval",
    )(xn, k1, k2, gate, u.reshape(2, n_exp // 2, d), v.reshape(2, n_exp // 2, d), h)


def kernel(x_prompt, x_sample, cache_k, cache_v, cache_kidx, state_conv_a, state_conv_b, state_pool,
           page_table, meta_tokens, w_in, conv_a_w, conv_a_b, ln_a_g, ln_a_b, conv_b_w, pool_w,
           pool_scale, w_out, norm_mix_g, norm_ffn_g, peer_wq, peer_subkeys, peer_u, peer_v, norm_final_g):
    b, seq, d = x_prompt.shape
    db, dt, _ = x_sample.shape
    depth = w_in.shape[0]
    tp = N_META + seq
    n_p, n_s = b * tp, db * dt
    n_tok = n_p + n_s
    n_pad = pl.cdiv(n_tok, PEER_TILE) * PEER_TILE
    bf = jnp.bfloat16
    past_len = page_table.shape[1] * PAGE_SIZE

    meta = jnp.broadcast_to(meta_tokens[None], (b, N_META, d))
    hp = jnp.concatenate([meta, x_prompt], axis=1).reshape(n_p, d)
    h = jnp.concatenate([hp, x_sample.reshape(n_s, d), jnp.zeros((n_pad - n_tok, d), jnp.float32)], axis=0)

    news_p, news_s = [], []
    for l in range(depth):
        lw = {'conv_a_w': conv_a_w[l], 'conv_a_b': conv_a_b[l], 'ln_a_g': ln_a_g[l], 'ln_a_b': ln_a_b[l],
              'conv_b_w': conv_b_w[l], 'pool_w': pool_w[l], 'pool_scale': pool_scale[l]}
        past = {'cache_k': cache_k, 'cache_v': cache_v, 'cache_kidx': cache_kidx, 'layer': l,
                'conv_a': state_conv_a[l], 'conv_b': state_conv_b[l], 'pool': state_pool[l],
                'page_table': page_table}
        n_mix = 5 * GROUP_W
        w_proj = jnp.concatenate([w_in[l][:, :n_mix], w_in[l][:, N_COLS - GROUP_W:], w_in[l][:, n_mix:N_COLS - GROUP_W]],
                                 axis=1).astype(bf)
        z = norm_matmul(h, norm_mix_g[l], w_proj)
        ys_p, new_p = mixer_block(z[:n_p].reshape(b, tp, N_COLS), lw, None, 0)
        ys_s, new_s = mixer_block(z[n_p:n_tok].reshape(db, dt, N_COLS), lw, past, past_len)
        ys = [jnp.concatenate([yp.reshape(n_p, GROUP_W), ysm.reshape(n_s, GROUP_W),
                               jnp.zeros((n_pad - n_tok, GROUP_W), jnp.float32)], axis=0)
              for yp, ysm in zip(ys_p, ys_s)]
        h = out_proj(h, ys, w_out[l].astype(bf))
        xn, k1, k2, gate = peer_route(h, norm_ffn_g[l], peer_wq[l].T.astype(bf),
                                      peer_subkeys[l].reshape(PEER_HEADS * 2, N_KEYS, PEER_HALF).astype(bf))
        h = peer_eval(xn, k1, k2, gate, peer_u[l].astype(bf), peer_v[l].astype(bf), h)
        news_p.append(new_p)
        news_s.append(new_s)

    y = rms_norm_rows(h, norm_final_g)
    y_prompt = y[:n_p].reshape(b, tp, d)[:, N_META:]
    y_sample = y[n_p:n_tok].reshape(db, dt, d)
    outs_p = [jnp.stack([n[i] for n in news_p]) for i in range(6)]
    outs_s = [jnp.stack([n[i] for n in news_s]) for i in range(6)]
    return (y_prompt, y_sample, *outs_p, *outs_s)
```

```python
import functools

import jax
import jax.numpy as jnp
from jax import lax
from jax.experimental import pallas as pl
from jax.experimental.pallas import tpu as pltpu

N_META = 16
GROUP_W = 256
CONV_A_WIDTH = 31
CONV_B_WIDTH = 3
N_HEADS = 4
HEAD_DIM = 64
N_IDX_HEADS = 8
IDX_DIM = 64
TOPK_MAX = 256
PAGE_SIZE = 128
POOL_WINDOWS = (2, 4, 8, 16)
POOL_GROUP = 64
POOL_CTX = 15
PEER_HEADS = 8
N_KEYS = 128
PEER_HALF = 128
PEER_TOPK = 16
N_ROUTES = PEER_HEADS * PEER_TOPK
RMS_EPS = 1e-6
LN_EPS = 1e-5
SPLIT_SIZES = (GROUP_W,) * 8 + (N_IDX_HEADS * IDX_DIM, IDX_DIM, N_IDX_HEADS, GROUP_W)
N_COLS = sum(SPLIT_SIZES)

Q_BLOCK = 128
KEY_CHUNK = 512
ROW_TILE = 256
ROUTE_TILE = 256
PEER_TILE = 512
HALF_KEYS = N_KEYS // 2
KEY_PAIRS_PER_STEP = 4
GATE_BUILD_UNROLL = 32
MIX_COLS = 6 * GROUP_W
MIX_TILE = 512
MIX_HALO = 32
SAMPLE_ROWS = 8
SAMPLE_GROUP = 2
PAGES_PER_CHUNK = 8
SAMPLE_CHUNK = PAGES_PER_CHUNK * PAGE_SIZE
INT_MIN = -2 ** 31
NEG_INF_KEY = -2139095041
VMEM_LIMIT_BYTES = 48 * 1024 * 1024
PEER_VMEM_LIMIT_BYTES = 56 * 1024 * 1024


def _compiler_params(vmem_limit_bytes=VMEM_LIMIT_BYTES, **kw):
    return pltpu.CompilerParams(vmem_limit_bytes=vmem_limit_bytes, **kw)


def _norm_matmul_kernel(x_ref, g_ref, w_ref, o_ref):
    x = x_ref[...]
    y = x * lax.rsqrt(jnp.mean(x * x, axis=-1, keepdims=True) + RMS_EPS) * g_ref[...]
    o_ref[...] = jnp.dot(y.astype(jnp.bfloat16), w_ref[...], preferred_element_type=jnp.float32)


def norm_matmul(x, g, w):
    n, d = x.shape
    c = w.shape[1]
    return pl.pallas_call(
        _norm_matmul_kernel,
        grid=(n // ROW_TILE,),
        in_specs=[pl.BlockSpec((ROW_TILE, d), lambda i: (i, 0)),
                  pl.BlockSpec((1, d), lambda i: (0, 0)),
                  pl.BlockSpec((d, c), lambda i: (0, 0))],
        out_specs=pl.BlockSpec((ROW_TILE, c), lambda i: (i, 0)),
        out_shape=jax.ShapeDtypeStruct((n, c), jnp.float32),
        compiler_params=_compiler_params(),
        name="norm_matmul",
    )(x, g.reshape(1, d), w)


def _out_proj_kernel(h_ref, ya_ref, yb_ref, yc_ref, yd_ref, w_ref, o_ref):
    acc = h_ref[...]
    for gi, y_ref in enumerate((ya_ref, yb_ref, yc_ref, yd_ref)):
        acc += jnp.dot(y_ref[...].astype(jnp.bfloat16), w_ref[gi * GROUP_W:(gi + 1) * GROUP_W, :],
                       preferred_element_type=jnp.float32)
    o_ref[...] = acc


def out_proj(h, ys, w):
    n, d = h.shape
    y_spec = pl.BlockSpec((ROW_TILE, GROUP_W), lambda i: (i, 0))
    return pl.pallas_call(
        _out_proj_kernel,
        grid=(n // ROW_TILE,),
        in_specs=[pl.BlockSpec((ROW_TILE, d), lambda i: (i, 0)), y_spec, y_spec, y_spec, y_spec,
                  pl.BlockSpec(w.shape, lambda i: (0, 0))],
        out_specs=pl.BlockSpec((ROW_TILE, d), lambda i: (i, 0)),
        out_shape=jax.ShapeDtypeStruct((n, d), jnp.float32),
        compiler_params=_compiler_params(),
        name="out_proj",
    )(h, *ys, w)


def _rms_norm_kernel(x_ref, g_ref, o_ref):
    x = x_ref[...]
    o_ref[...] = x * lax.rsqrt(jnp.mean(x * x, axis=-1, keepdims=True) + RMS_EPS) * g_ref[...]


def rms_norm_rows(x, g):
    n, d = x.shape
    return pl.pallas_call(
        _rms_norm_kernel,
        grid=(n // ROW_TILE,),
        in_specs=[pl.BlockSpec((ROW_TILE, d), lambda i: (i, 0)), pl.BlockSpec((1, d), lambda i: (0, 0))],
        out_specs=pl.BlockSpec((ROW_TILE, d), lambda i: (i, 0)),
        out_shape=jax.ShapeDtypeStruct((n, d), jnp.float32),
        compiler_params=_compiler_params(),
        name="final_norm",
    )(x, g.reshape(1, d))


def _sortable_key(x):
    x = jnp.where(x == 0.0, 0.0, x)
    b = pltpu.bitcast(x, jnp.int32)
    return b ^ ((b >> 31) & 0x7FFFFFFF)


def _col_sum(x):
    return jnp.sum(x.reshape(x.shape[0] // 8, 8, x.shape[1]), axis=0)


def _dsa_prompt_kernel(qt_ref, qit_ref, wt_ref, kidx_ref, k_ref, vt_ref, o_ref, key_ref, *, topk):
    i = pl.program_id(1)
    n_chunks = pl.cdiv((i + 1) * Q_BLOCK, KEY_CHUNK)
    q_pos = i * Q_BLOCK + lax.broadcasted_iota(jnp.int32, (KEY_CHUNK, Q_BLOCK), 1)
    key_row = lax.broadcasted_iota(jnp.int32, (KEY_CHUNK, Q_BLOCK), 0)
    qit = qit_ref[...]
    wt = wt_ref[...]

    def score_chunk(c, carry):
        off = pl.multiple_of(c * KEY_CHUNK, KEY_CHUNK)
        s = jnp.dot(kidx_ref[pl.ds(off, KEY_CHUNK), :], qit, preferred_element_type=jnp.float32)
        s = jnp.maximum(s, 0.0)
        score = wt[0:1] * s[:, 0:Q_BLOCK]
        for h in range(1, N_IDX_HEADS):
            score = score + wt[h:h + 1] * s[:, h * Q_BLOCK:(h + 1) * Q_BLOCK]
        visible = (off + key_row) <= q_pos
        key_ref[c] = jnp.where(visible, _sortable_key(score), NEG_INF_KEY)
        return carry

    lax.fori_loop(0, n_chunks, score_chunk, 0)

    def count(pred):
        def body(c, acc):
            p = pred(key_ref[c]).astype(jnp.float32)
            return acc + p[:KEY_CHUNK // 2] + p[KEY_CHUNK // 2:]
        acc = lax.fori_loop(0, n_chunks, body, jnp.zeros((KEY_CHUNK // 2, Q_BLOCK), jnp.float32))
        return jnp.sum(acc, axis=0, keepdims=True)

    kf = float(topk)
    t0 = jnp.where(count(lambda k: k >= 0) >= kf, 0, INT_MIN).astype(jnp.int32)

    def search(it, t):
        cand = t + (jnp.int32(1) << (30 - it))
        return jnp.where(count(lambda k: k >= cand) >= kf, cand, t)

    thr = lax.fori_loop(0, 31, search, t0)
    n_take_eq = kf - count(lambda k: k > thr)

    d_head = lax.broadcasted_iota(jnp.int32, (GROUP_W, Q_BLOCK), 0) // HEAD_DIM
    qt = qt_ref[...]
    q_heads = jnp.concatenate([jnp.where(d_head == h, qt, jnp.zeros_like(qt)) for h in range(N_HEADS)], axis=1)
    tri = (lax.broadcasted_iota(jnp.int32, (KEY_CHUNK, KEY_CHUNK), 1)
           < lax.broadcasted_iota(jnp.int32, (KEY_CHUNK, KEY_CHUNK), 0)).astype(jnp.bfloat16)

    def attend_chunk(c, carry):
        eq_before, m, l, acc = carry
        off = pl.multiple_of(c * KEY_CHUNK, KEY_CHUNK)
        keys = key_ref[c]
        eq = keys == thr
        eq_excl = jnp.dot(tri, eq.astype(jnp.bfloat16), preferred_element_type=jnp.float32)
        sel = (keys > thr) | (eq & ((eq_before + eq_excl) < n_take_eq))
        sel = sel & (keys != NEG_INF_KEY)
        lg_all = jnp.dot(k_ref[pl.ds(off, KEY_CHUNK), :], q_heads,
                         preferred_element_type=jnp.float32) * (HEAD_DIM ** -0.5)
        vt = vt_ref[c]
        m_rows, l_rows, acc_rows = [], [], []
        for h in range(N_HEADS):
            lg = jnp.where(sel, lg_all[:, h * Q_BLOCK:(h + 1) * Q_BLOCK], -jnp.inf)
            m_old = m[h:h + 1]
            m_new = jnp.maximum(m_old, jnp.max(lg, axis=0, keepdims=True))
            m_safe = jnp.where(m_new == -jnp.inf, 0.0, m_new)
            p = jnp.exp(lg - m_safe)
            alpha = jnp.exp(m_old - m_safe)
            l_rows.append(alpha * l[h:h + 1] + jnp.sum(_col_sum(p), axis=0, keepdims=True))
            m_rows.append(m_new)
            pv = jnp.dot(vt[h * HEAD_DIM:(h + 1) * HEAD_DIM], p.astype(jnp.bfloat16),
                         preferred_element_type=jnp.float32)
            acc_rows.append(alpha * acc[h * HEAD_DIM:(h + 1) * HEAD_DIM] + pv)
        eq_after = eq_before + jnp.sum(_col_sum(eq.astype(jnp.float32)), axis=0, keepdims=True)
        return (eq_after, jnp.concatenate(m_rows, axis=0), jnp.concatenate(l_rows, axis=0),
                jnp.concatenate(acc_rows, axis=0))

    init = (jnp.zeros((1, Q_BLOCK), jnp.float32), jnp.full((N_HEADS, Q_BLOCK), -jnp.inf, jnp.float32),
            jnp.zeros((N_HEADS, Q_BLOCK), jnp.float32), jnp.zeros((GROUP_W, Q_BLOCK), jnp.float32))
    _, _, l, acc = lax.fori_loop(0, n_chunks, attend_chunk, init)
    out_t = jnp.concatenate([acc[h * HEAD_DIM:(h + 1) * HEAD_DIM] / l[h:h + 1] for h in range(N_HEADS)], axis=0)
    o_ref[...] = out_t.T


def dsa_prompt(q, k, v, q_idx, k_idx, w_idx, topk):
    b, t, _ = q.shape
    n_blk = pl.cdiv(t, Q_BLOCK)
    tq = n_blk * Q_BLOCK
    tk = pl.cdiv(tq, KEY_CHUNK) * KEY_CHUNK
    n_ch = tk // KEY_CHUNK
    bf = jnp.bfloat16
    pad_q = lambda a: jnp.pad(a, [(0, 0), (0, tq - t), (0, 0)])
    pad_k = lambda a: jnp.pad(a, [(0, 0), (0, tk - t), (0, 0)])
    qt = pad_q(q).astype(bf).reshape(b, n_blk, Q_BLOCK, GROUP_W).transpose(0, 1, 3, 2)
    qit = (pad_q(q_idx).astype(bf).reshape(b, n_blk, Q_BLOCK, N_IDX_HEADS, IDX_DIM)
           .transpose(0, 1, 4, 3, 2).reshape(b, n_blk, IDX_DIM, N_IDX_HEADS * Q_BLOCK))
    wt = pad_q(w_idx).reshape(b, n_blk, Q_BLOCK, N_IDX_HEADS).transpose(0, 1, 3, 2)
    vt = pad_k(v).astype(bf).reshape(b, n_ch, KEY_CHUNK, GROUP_W).transpose(0, 1, 3, 2)
    out = pl.pallas_call(
        functools.partial(_dsa_prompt_kernel, topk=topk),
        grid=(b, n_blk),
        in_specs=[
            pl.BlockSpec((None, None, GROUP_W, Q_BLOCK), lambda bi, i: (bi, i, 0, 0)),
            pl.BlockSpec((None, None, IDX_DIM, N_IDX_HEADS * Q_BLOCK), lambda bi, i: (bi, i, 0, 0)),
            pl.BlockSpec((None, None, N_IDX_HEADS, Q_BLOCK), lambda bi, i: (bi, i, 0, 0)),
            pl.BlockSpec((None, tk, IDX_DIM), lambda bi, i: (bi, 0, 0)),
            pl.BlockSpec((None, tk, GROUP_W), lambda bi, i: (bi, 0, 0)),
            pl.BlockSpec((None, n_ch, GROUP_W, KEY_CHUNK), lambda bi, i: (bi, 0, 0, 0)),
        ],
        out_specs=pl.BlockSpec((None, Q_BLOCK, GROUP_W), lambda bi, i: (bi, i, 0)),
        out_shape=jax.ShapeDtypeStruct((b, tq, GROUP_W), jnp.float32),
        scratch_shapes=[pltpu.VMEM((n_ch, KEY_CHUNK, Q_BLOCK), jnp.int32)],
        compiler_params=_compiler_params(),
        name="dsa_prompt",
    )(qt, qit, wt, pad_k(k_idx).astype(bf), pad_k(k).astype(bf), vt)
    return out[:, :t]


def _dsa_sample_kernel(pt_ref, q_ref, qi_ref, w_ref, kin_ref, kn_ref, vn_ref, kidx_hbm, k_hbm, v_hbm, o_ref,
                       kidx_buf, k_buf, v_buf, key_ref, m_ref, l_ref, acc_ref, sems, *, layer, n_pages, topk):
    step = pl.program_id(0)
    n_chunks = key_ref.shape[0]
    n_buf_pages = kidx_buf.shape[1]
    rows = SAMPLE_ROWS
    group = SAMPLE_GROUP

    def page_copy(g, p, which):
        src, dst = ((kidx_hbm, kidx_buf), (k_hbm, k_buf), (v_hbm, v_buf))[which]
        return pltpu.make_async_copy(src.at[layer, pt_ref[step * group + g, p]], dst.at[g, p], sems.at[which])

    def start_page(p, carry):
        for g in range(group):
            for which in range(3):
                page_copy(g, p, which).start()
        return carry

    lax.fori_loop(0, n_pages, start_page, 0)

    for g in range(group):
        for p in range(n_pages, n_buf_pages):
            for buf, new_ref in ((kidx_buf, kin_ref), (k_buf, kn_ref), (v_buf, vn_ref)):
                buf[g, p] = new_ref[g].astype(buf.dtype) if p == n_pages else jnp.zeros(buf.shape[2:], buf.dtype)

    def wait_pages(which):
        def body(p, carry):
            for g in range(group):
                page_copy(g, p, which).wait()
            return carry
        lax.fori_loop(0, n_pages, body, 0)

    wait_pages(0)

    past_len = n_pages * PAGE_SIZE
    q_row = lax.broadcasted_iota(jnp.int32, (group * rows, SAMPLE_CHUNK), 0) % rows
    lane = lax.broadcasted_iota(jnp.int32, (group * rows, SAMPLE_CHUNK), 1)
    qi = [qi_ref[g].reshape(N_IDX_HEADS * rows, IDX_DIM) for g in range(group)]
    w_cols = [[jnp.broadcast_to(w_ref[g][:, h:h + 1], (rows, SAMPLE_CHUNK)) for h in range(N_IDX_HEADS)]
              for g in range(group)]

    def score_chunk(c, carry):
        p0 = pl.multiple_of(c * PAGES_PER_CHUNK, PAGES_PER_CHUNK)
        scores = []
        for g in range(group):
            kc = kidx_buf[g, pl.ds(p0, PAGES_PER_CHUNK)].reshape(SAMPLE_CHUNK, IDX_DIM)
            s = lax.dot_general(qi[g], kc, (((1,), (1,)), ((), ())), preferred_element_type=jnp.float32)
            s = jnp.maximum(s, 0.0)
            score = w_cols[g][0] * s[0:rows]
            for h in range(1, N_IDX_HEADS):
                score = score + w_cols[g][h] * s[h * rows:(h + 1) * rows]
            scores.append(score)
        visible = (c * SAMPLE_CHUNK + lane) <= (past_len + q_row)
        key_ref[c] = jnp.where(visible, _sortable_key(jnp.concatenate(scores, axis=0)), NEG_INF_KEY)
        return carry

    lax.fori_loop(0, n_chunks, score_chunk, 0)

    def count(pred):
        def body(c, acc):
            return acc + pred(key_ref[c]).astype(jnp.float32)
        acc = lax.fori_loop(0, n_chunks, body, jnp.zeros((group * rows, SAMPLE_CHUNK), jnp.float32))
        return jnp.sum(acc, axis=-1, keepdims=True)

    kf = float(topk)
    t0 = jnp.where(count(lambda k: k >= 0) >= kf, 0, INT_MIN).astype(jnp.int32)

    def search(it, t):
        cand = t + (jnp.int32(1) << (30 - it))
        return jnp.where(count(lambda k: k >= cand) >= kf, cand, t)

    thr = lax.fori_loop(0, 31, search, t0)
    n_take_eq = kf - count(lambda k: k > thr)

    wait_pages(1)
    wait_pages(2)

    head_of_lane = lax.broadcasted_iota(jnp.int32, (rows, GROUP_W), 1) // HEAD_DIM
    q_heads = [jnp.concatenate([jnp.where(head_of_lane == h, q_ref[g], jnp.zeros((rows, GROUP_W), q_ref.dtype))
                                for h in range(N_HEADS)], axis=0) for g in range(group)]
    tri = (lax.broadcasted_iota(jnp.int32, (SAMPLE_CHUNK, SAMPLE_CHUNK), 0)
           < lax.broadcasted_iota(jnp.int32, (SAMPLE_CHUNK, SAMPLE_CHUNK), 1)).astype(jnp.bfloat16)

    m_ref[...] = jnp.full(m_ref.shape, -jnp.inf, jnp.float32)
    l_ref[...] = jnp.zeros(l_ref.shape, jnp.float32)
    acc_ref[...] = jnp.zeros(acc_ref.shape, jnp.float32)
    hr = N_HEADS * rows

    def attend_chunk(c, eq_before):
        p0 = pl.multiple_of(c * PAGES_PER_CHUNK, PAGES_PER_CHUNK)
        keys = key_ref[c]
        eq = keys == thr
        eq_excl = jnp.dot(eq.astype(jnp.bfloat16), tri, preferred_element_type=jnp.float32)
        sel = (keys > thr) | (eq & ((eq_before + eq_excl) < n_take_eq))
        sel = sel & (keys != NEG_INF_KEY)
        lgs, vcs = [], []
        for g in range(group):
            kc = k_buf[g, pl.ds(p0, PAGES_PER_CHUNK)].reshape(SAMPLE_CHUNK, GROUP_W)
            vcs.append(v_buf[g, pl.ds(p0, PAGES_PER_CHUNK)].reshape(SAMPLE_CHUNK, GROUP_W))
            lg = lax.dot_general(q_heads[g], kc, (((1,), (1,)), ((), ())),
                                 preferred_element_type=jnp.float32) * (HEAD_DIM ** -0.5)
            sel_g = jnp.concatenate([sel[g * rows:(g + 1) * rows]] * N_HEADS, axis=0)
            lgs.append(jnp.where(sel_g, lg, -jnp.inf))
        lg = jnp.concatenate(lgs, axis=0)
        m_old = m_ref[...]
        m_new = jnp.maximum(m_old, jnp.max(lg, axis=-1, keepdims=True))
        m_safe = jnp.where(m_new == -jnp.inf, 0.0, m_new)
        p = jnp.exp(lg - m_safe)
        alpha = jnp.exp(m_old - m_safe)
        l_ref[...] = alpha * l_ref[...] + jnp.sum(p, axis=-1, keepdims=True)
        m_ref[...] = m_new
        pb = p.astype(jnp.bfloat16)
        pv = jnp.concatenate([jnp.dot(pb[g * hr:(g + 1) * hr], vcs[g], preferred_element_type=jnp.float32)
                              for g in range(group)], axis=0)
        acc_ref[...] = alpha * acc_ref[...] + pv
        return eq_before + jnp.sum(eq.astype(jnp.float32), axis=-1, keepdims=True)

    lax.fori_loop(0, n_chunks, attend_chunk, jnp.zeros((group * rows, 1), jnp.float32))

    res = acc_ref[...] / l_ref[...]
    for g in range(group):
        out = jnp.zeros((rows, GROUP_W), jnp.float32)
        for h in range(N_HEADS):
            out = jnp.where(head_of_lane == h, res[g * hr + h * rows:g * hr + (h + 1) * rows], out)
        o_ref[g] = out


def dsa_sample(q, k, v, q_idx, k_idx, w_idx, cache_k, cache_v, cache_kidx, page_table, layer):
    db, t, _ = q.shape
    n_pages = page_table.shape[1]
    depth, n_pool = cache_k.shape[:2]
    topk = min(TOPK_MAX, (n_pages * PAGE_SIZE + t) // 4)
    rows = SAMPLE_ROWS
    group = SAMPLE_GROUP
    assert db % group == 0
    n_chunks = pl.cdiv(n_pages + 1, PAGES_PER_CHUNK)
    n_buf_pages = n_chunks * PAGES_PER_CHUNK
    bf = jnp.bfloat16
    pad_rows = lambda a, r: jnp.pad(a, [(0, 0), (0, r - t), (0, 0)])
    qi = pad_rows(q_idx, rows).astype(bf).reshape(db, rows, N_IDX_HEADS, IDX_DIM).transpose(0, 2, 1, 3)
    per_g = lambda *shape: pl.BlockSpec((group,) + shape, lambda s, pt: (s,) + (0,) * len(shape))
    hbm = pl.BlockSpec(memory_space=pl.ANY)
    grid_spec = pltpu.PrefetchScalarGridSpec(
        num_scalar_prefetch=1,
        grid=(db // group,),
        in_specs=[per_g(rows, GROUP_W), per_g(N_IDX_HEADS, rows, IDX_DIM), per_g(rows, N_IDX_HEADS),
                  per_g(PAGE_SIZE, IDX_DIM), per_g(PAGE_SIZE, GROUP_W), per_g(PAGE_SIZE, GROUP_W),
                  hbm, hbm, hbm],
        out_specs=per_g(rows, GROUP_W),
        scratch_shapes=[
            pltpu.VMEM((group, n_buf_pages, PAGE_SIZE, IDX_DIM), bf),
            pltpu.VMEM((group, n_buf_pages, PAGE_SIZE, GROUP_W), bf),
            pltpu.VMEM((group, n_buf_pages, PAGE_SIZE, GROUP_W), bf),
            pltpu.VMEM((n_chunks, group * rows, SAMPLE_CHUNK), jnp.int32),
            pltpu.VMEM((group * N_HEADS * rows, 1), jnp.float32),
            pltpu.VMEM((group * N_HEADS * rows, 1), jnp.float32),
            pltpu.VMEM((group * N_HEADS * rows, GROUP_W), jnp.float32),
            pltpu.SemaphoreType.DMA((3,)),
        ],
    )
    out = pl.pallas_call(
        functools.partial(_dsa_sample_kernel, layer=layer, n_pages=n_pages, topk=topk),
        grid_spec=grid_spec,
        out_shape=jax.ShapeDtypeStruct((db, rows, GROUP_W), jnp.float32),
        compiler_params=_compiler_params(),
        name="dsa_sample",
    )(page_table, pad_rows(q, rows).astype(bf), qi, pad_rows(w_idx, rows),
      pad_rows(k_idx, PAGE_SIZE), pad_rows(k, PAGE_SIZE), pad_rows(v, PAGE_SIZE),
      cache_kidx.astype(bf), cache_k.astype(bf).reshape(depth, n_pool, PAGE_SIZE, GROUP_W),
      cache_v.astype(bf).reshape(depth, n_pool, PAGE_SIZE, GROUP_W))
    return out[:, :t]


def _mixer_streams(z):
    a_val, a_gate, b_gate, c_gate, b_in, pool_in = [z[:, g * GROUP_W:(g + 1) * GROUP_W] for g in range(6)]
    return a_val * jax.nn.sigmoid(a_gate), c_gate * b_in, pool_in, b_gate


def _mixers_kernel(z_ref, ctx_ref, wa_ref, ba_ref, lng_ref, lnb_ref, wb_ref, wp_ref, ps_ref,
                   ya_ref, yb_ref, yd_ref, ta_ref, tb_ref, tp_ref, ext_ref, *, ctx_from_z, start_pos, n_valid_last):
    i = pl.program_id(1)
    tt = z_ref.shape[0]
    u, g_in, p_in, b_gate = _mixer_streams(z_ref[...])
    if ctx_from_z:
        hu, hg, hp, _ = _mixer_streams(ctx_ref[...])
        hu, hg, hp = [jnp.where(i > 0, a, 0.0) for a in (hu, hg, hp)]
    else:
        hu, hg, hp = [ctx_ref[:, g * GROUP_W:(g + 1) * GROUP_W] for g in range(3)]
    for s, (head, body) in enumerate(((hu, u), (hg, g_in), (hp, p_in))):
        ext_ref[s, 0:MIX_HALO] = head
        ext_ref[s, MIX_HALO:MIX_HALO + tt] = body

    def rows(s, back):
        return ext_ref[s, pl.ds(MIX_HALO - back, tt), :]

    acc = jnp.zeros((tt, GROUP_W), jnp.float32)
    for j in range(CONV_A_WIDTH):
        acc = acc + wa_ref[j:j + 1, :] * rows(0, CONV_A_WIDTH - 1 - j)
    acc = acc + ba_ref[...]
    mu = jnp.mean(acc, axis=-1, keepdims=True)
    var = jnp.mean(jnp.square(acc - mu), axis=-1, keepdims=True)
    ln = (acc - mu) * lax.rsqrt(var + LN_EPS) * lng_ref[...] + lnb_ref[...]
    ya_ref[...] = ln * jax.nn.sigmoid(ln)

    cb = jnp.zeros((tt, GROUP_W), jnp.float32)
    for j in range(CONV_B_WIDTH):
        cb = cb + wb_ref[j:j + 1, :] * rows(1, CONV_B_WIDTH - 1 - j)
    yb_ref[...] = b_gate * cb

    lane_group = lax.broadcasted_iota(jnp.int32, (tt, GROUP_W), 1) // POOL_GROUP
    pos1 = start_pos + i * tt + lax.broadcasted_iota(jnp.int32, (tt, GROUP_W), 0) + 1
    run = p_in
    mean = jnp.zeros((tt, GROUP_W), jnp.float32)
    back = 1
    for gi, w in enumerate(POOL_WINDOWS):
        while back < w:
            run = run + rows(2, back)
            back += 1
        cnt = jnp.minimum(pos1, w).astype(jnp.float32)
        mean = jnp.where(lane_group == gi, run / cnt, mean)
    pooled = (mean - p_in).astype(jnp.bfloat16)
    yd_ref[...] = jnp.dot(pooled, wp_ref[...], preferred_element_type=jnp.float32) * ps_ref[...]

    @pl.when(i == pl.num_programs(1) - 1)
    def _():
        end = MIX_HALO + n_valid_last
        ta_ref[...] = ext_ref[0, end - (CONV_A_WIDTH - 1):end, :]
        tb_ref[...] = ext_ref[1, end - (CONV_B_WIDTH - 1):end, :]
        tp_ref[...] = ext_ref[2, end - POOL_CTX:end, :]


def mixers(z, ctx, lw, start_pos):
    b, t, _ = z.shape
    tt = min(MIX_TILE, t)
    n_tiles = pl.cdiv(t, tt)
    ctx_from_z = ctx is None
    if ctx_from_z:
        per_tile = tt // MIX_HALO
        ctx_arr = z
        ctx_spec = pl.BlockSpec((None, MIX_HALO, MIX_COLS), lambda bi, i: (bi, jnp.maximum(i * per_tile - 1, 0), 0))
    else:
        ctx_arr = ctx
        ctx_spec = pl.BlockSpec((None, MIX_HALO, 3 * GROUP_W), lambda bi, i: (bi, 0, 0))
    wp = jnp.zeros((GROUP_W, GROUP_W), jnp.float32)
    for gi in range(len(POOL_WINDOWS)):
        wp = wp.at[gi * POOL_GROUP:(gi + 1) * POOL_GROUP, gi * POOL_GROUP:(gi + 1) * POOL_GROUP].set(lw['pool_w'][gi])
    row = lambda a: a.reshape(1, GROUP_W)
    const = lambda shape: pl.BlockSpec(shape, lambda bi, i: (0,) * len(shape))
    y_spec = pl.BlockSpec((None, tt, GROUP_W), lambda bi, i: (bi, i, 0))
    tail = lambda r: pl.BlockSpec((None, r, GROUP_W), lambda bi, i: (bi, 0, 0))
    f32 = jnp.float32
    return pl.pallas_call(
        functools.partial(_mixers_kernel, ctx_from_z=ctx_from_z, start_pos=start_pos,
                          n_valid_last=t - (n_tiles - 1) * tt),
        grid=(b, n_tiles),
        in_specs=[pl.BlockSpec((None, tt, MIX_COLS), lambda bi, i: (bi, i, 0)), ctx_spec,
                  const((CONV_A_WIDTH, GROUP_W)), const((1, GROUP_W)), const((1, GROUP_W)), const((1, GROUP_W)),
                  const((CONV_B_WIDTH, GROUP_W)), const((GROUP_W, GROUP_W)), const((1, GROUP_W))],
        out_specs=[y_spec, y_spec, y_spec, tail(CONV_A_WIDTH - 1), tail(CONV_B_WIDTH - 1), tail(POOL_CTX)],
        out_shape=[jax.ShapeDtypeStruct((b, t, GROUP_W), f32)] * 3
                  + [jax.ShapeDtypeStruct((b, r, GROUP_W), f32) for r in (CONV_A_WIDTH - 1, CONV_B_WIDTH - 1, POOL_CTX)],
        scratch_shapes=[pltpu.VMEM((3, MIX_HALO + tt, GROUP_W), f32)],
        compiler_params=_compiler_params(),
        name="mixers",
    )(z, ctx_arr, lw['conv_a_w'], row(lw['conv_a_b']), row(lw['ln_a_g']), row(lw['ln_a_b']),
      lw['conv_b_w'], wp.astype(jnp.bfloat16), row(lw['pool_scale']))


def mixer_block(z, lw, past, start_pos):
    b, t, _ = z.shape
    parts, start = [], MIX_COLS
    for size in (GROUP_W, GROUP_W, GROUP_W, N_IDX_HEADS * IDX_DIM, IDX_DIM, N_IDX_HEADS):
        parts.append(z[..., start:start + size])
        start += size
    q, k, v, q_idx, k_idx, w_idx = parts
    if past is None:
        ctx = None
        y_c = dsa_prompt(q, k, v, q_idx, k_idx, w_idx, min(TOPK_MAX, t // 4))
    else:
        front = lambda a: jnp.pad(a, [(0, 0), (MIX_HALO - a.shape[1], 0), (0, 0)])
        ctx = jnp.concatenate([front(past['conv_a']), front(past['conv_b']), front(past['pool'])], axis=-1)
        y_c = dsa_sample(q, k, v, q_idx, k_idx, w_idx, past['cache_k'], past['cache_v'], past['cache_kidx'],
                         past['page_table'], past['layer'])
    y_a, y_b, y_d, tail_a, tail_b, tail_p = mixers(z, ctx, lw, start_pos)
    new = (k.reshape(b, t, N_HEADS, HEAD_DIM), v.reshape(b, t, N_HEADS, HEAD_DIM), k_idx, tail_a, tail_b, tail_p)
    return (y_a, y_b, y_c, y_d), new


def _top_rows(s, k):
    r = s.shape[0]
    row = lax.broadcasted_iota(jnp.int32, s.shape, 0)
    vals, idxs = [], []
    for _ in range(k):
        m = jnp.max(s, axis=0, keepdims=True)
        idx = jnp.min(jnp.where(s == m, row, r), axis=0, keepdims=True)
        s = jnp.where(row == idx, -jnp.inf, s)
        vals.append(m)
        idxs.append(idx)
    return jnp.concatenate(vals, axis=0), jnp.concatenate(idxs, axis=0)


def _pick_rows(table, which):
    row = lax.broadcasted_iota(jnp.int32, table.shape, 0)
    out = []
    for j in range(which.shape[0]):
        out.append(jnp.sum(jnp.where(row == which[j:j + 1], table, 0), axis=0, keepdims=True))
    return jnp.concatenate(out, axis=0)


CAND_MID_B = 8
CAND_MID_START = PEER_TOPK
CAND_TAIL_START = CAND_MID_START + 7 * CAND_MID_B


def _cand_a(r):
    return jnp.where(r < CAND_MID_START, 0,
                     jnp.where(r < CAND_TAIL_START, (r - CAND_MID_START) // CAND_MID_B + 1,
                               r - CAND_TAIL_START + CAND_MID_B))


def _cand_b(r):
    return jnp.where(r < CAND_MID_START, r, jnp.where(r < CAND_TAIL_START, (r - CAND_MID_START) % CAND_MID_B, 0))


def _peer_route_kernel(h_ref, g_ref, wqt_ref, sk_ref, xn_ref, k1_ref, k2_ref, gate_ref,
                       qt_ref, k1t_ref, k2t_ref, gt_ref):
    x = h_ref[...]
    xn = (x * lax.rsqrt(jnp.mean(x * x, axis=-1, keepdims=True) + RMS_EPS) * g_ref[...]).astype(jnp.bfloat16)
    xn_ref[...] = xn
    qt_ref[...] = lax.dot_general(wqt_ref[...], xn, (((1,), (1,)), ((), ())),
                                  preferred_element_type=jnp.float32).astype(jnp.bfloat16)

    def head(h, carry):
        r0 = pl.multiple_of(h * 2 * PEER_HALF, 2 * PEER_HALF)
        s1 = jnp.dot(sk_ref[2 * h], qt_ref[pl.ds(r0, PEER_HALF), :], preferred_element_type=jnp.float32)
        s2 = jnp.dot(sk_ref[2 * h + 1], qt_ref[pl.ds(r0 + PEER_HALF, PEER_HALF), :],
                     preferred_element_type=jnp.float32)
        v1, i1 = _top_rows(s1, PEER_TOPK)
        v2, i2 = _top_rows(s2, PEER_TOPK)
        cand = jnp.concatenate([v1[0:1] + v2] + [v1[a:a + 1] + v2[0:CAND_MID_B] for a in range(1, CAND_MID_B)]
                               + [v1[CAND_MID_B:PEER_TOPK] + v2[0:1]], axis=0)
        r = lax.broadcasted_iota(jnp.int32, cand.shape, 0)
        cand = jnp.where((_cand_a(r) + 1) * (_cand_b(r) + 1) <= PEER_TOPK, cand, -jnp.inf)
        best, ridx = _top_rows(cand, PEER_TOPK)
        e = jnp.exp(best - best[0:1])
        o0 = pl.multiple_of(h * PEER_TOPK, PEER_TOPK)
        k1t_ref[pl.ds(o0, PEER_TOPK), :] = _pick_rows(i1, _cand_a(ridx))
        k2t_ref[pl.ds(o0, PEER_TOPK), :] = _pick_rows(i2, _cand_b(ridx))
        gt_ref[pl.ds(o0, PEER_TOPK), :] = e / jnp.sum(e, axis=0, keepdims=True)
        return carry

    lax.fori_loop(0, PEER_HEADS, head, 0)
    k1_ref[...] = k1t_ref[...].T
    k2_ref[...] = k2t_ref[...].T
    gate_ref[...] = gt_ref[...].T


def peer_route(h, g, wqt, sk):
    n, d = h.shape
    tm = ROUTE_TILE
    row_spec = pl.BlockSpec((tm, N_ROUTES), lambda i: (i, 0))
    return pl.pallas_call(
        _peer_route_kernel,
        grid=(n // tm,),
        in_specs=[pl.BlockSpec((tm, d), lambda i: (i, 0)),
                  pl.BlockSpec((1, d), lambda i: (0, 0)),
                  pl.BlockSpec(wqt.shape, lambda i: (0, 0)),
                  pl.BlockSpec(sk.shape, lambda i: (0, 0, 0))],
        out_specs=[pl.BlockSpec((tm, d), lambda i: (i, 0)), row_spec, row_spec, row_spec],
        out_shape=[jax.ShapeDtypeStruct((n, d), jnp.bfloat16),
                   jax.ShapeDtypeStruct((n, N_ROUTES), jnp.int32),
                   jax.ShapeDtypeStruct((n, N_ROUTES), jnp.int32),
                   jax.ShapeDtypeStruct((n, N_ROUTES), jnp.float32)],
        scratch_shapes=[pltpu.VMEM((PEER_HEADS * 2 * PEER_HALF, tm), jnp.bfloat16),
                        pltpu.VMEM((N_ROUTES, tm), jnp.int32),
                        pltpu.VMEM((N_ROUTES, tm), jnp.int32),
                        pltpu.VMEM((N_ROUTES, tm), jnp.float32)],
        compiler_params=_compiler_params(),
        name="peer_route",
    )(h, g.reshape(1, d), wqt, sk)


def _peer_eval_kernel(xn_ref, k1_ref, k2_ref, gate_ref, u_ref, v_ref, h_ref, o_ref, w_ref):
    j = pl.program_id(1)
    tn = xn_ref.shape[0]

    @pl.when(j == 0)
    def _():
        sub = lax.broadcasted_iota(jnp.int32, (N_KEYS, N_ROUTES), 0)

        def token(n, carry):
            k1 = jnp.broadcast_to(k1_ref[pl.ds(n, 1), :], (N_KEYS, N_ROUTES))
            k2 = jnp.broadcast_to(k2_ref[pl.ds(n, 1), :], (N_KEYS, N_ROUTES))
            gt = jnp.broadcast_to(gate_ref[pl.ds(n, 1), :], (N_KEYS, N_ROUTES))
            left = jnp.where(sub == k1, gt, 0.0).astype(jnp.bfloat16)
            right = (sub == k2).astype(jnp.bfloat16)
            tile = lax.dot_general(left, right, (((1,), (1,)), ((), ())), preferred_element_type=jnp.float32)
            tile = tile.astype(jnp.bfloat16).astype(jnp.float32)
            lo = lax.shift_right_logical(pltpu.bitcast(tile[:HALF_KEYS], jnp.int32), 16)
            hi = pltpu.bitcast(tile[HALF_KEYS:], jnp.int32) & jnp.int32(-65536)
            w_ref[pl.ds(pl.multiple_of(n * HALF_KEYS, HALF_KEYS), HALF_KEYS), :] = hi | lo
            return carry

        lax.fori_loop(0, tn, token, 0, unroll=GATE_BUILD_UNROLL)
        o_ref[...] = h_ref[...]

    eb = KEY_PAIRS_PER_STEP * N_KEYS
    u_blk = u_ref[...].reshape(2 * eb, u_ref.shape[-1])
    v_blk = v_ref[...].reshape(2 * eb, v_ref.shape[-1])
    act = lax.dot_general(xn_ref[...], u_blk, (((1,), (1,)), ((), ())), preferred_element_type=jnp.float32)
    words = [w_ref[pl.ds(j * KEY_PAIRS_PER_STEP + c, tn, stride=HALF_KEYS), :] for c in range(KEY_PAIRS_PER_STEP)]
    wgt = jnp.concatenate([pltpu.bitcast(wd << 16, jnp.float32) for wd in words]
                          + [pltpu.bitcast(wd & jnp.int32(-65536), jnp.float32) for wd in words], axis=1)
    coef = (wgt * jax.nn.gelu(act)).astype(jnp.bfloat16)
    o_ref[...] += jnp.dot(coef, v_blk, preferred_element_type=jnp.float32)


def peer_eval(xn, k1, k2, gate, u, v, h):
    n, d = xn.shape
    tn = PEER_TILE
    eb = KEY_PAIRS_PER_STEP * N_KEYS
    n_exp = u.shape[0]
    tok = lambda i, j: (i, 0)
    halves = pl.BlockSpec((2, eb, d), lambda i, j: (0, j, 0))
    return pl.pallas_call(
        _peer_eval_kernel,
        grid=(n // tn, n_exp // (2 * eb)),
        in_specs=[pl.BlockSpec((tn, d), tok),
                  pl.BlockSpec((tn, N_ROUTES), tok), pl.BlockSpec((tn, N_ROUTES), tok),
                  pl.BlockSpec((tn, N_ROUTES), tok),
                  halves, halves,
                  pl.BlockSpec((tn, d), tok)],
        out_specs=pl.BlockSpec((tn, d), tok),
        out_shape=jax.ShapeDtypeStruct((n, d), jnp.float32),
        scratch_shapes=[pltpu.VMEM((tn * HALF_KEYS, N_KEYS), jnp.int32)],
        compiler_params=_compiler_params(PEER_VMEM_LIMIT_BYTES, dimension_semantics=("arbitrary", "arbitrary")),
        name="peer_eval",
    )(xn, k1, k2, gate, u.reshape(2, n_exp // 2, d), v.reshape(2, n_exp // 2, d), h)


def kernel(x_prompt, x_sample, cache_k, cache_v, cache_kidx, state_conv_a, state_conv_b, state_pool,
           page_table, meta_tokens, w_in, conv_a_w, conv_a_b, ln_a_g, ln_a_b, conv_b_w, pool_w,
           pool_scale, w_out, norm_mix_g, norm_ffn_g, peer_wq, peer_subkeys, peer_u, peer_v, norm_final_g):
    b, seq, d = x_prompt.shape
    db, dt, _ = x_sample.shape
    depth = w_in.shape[0]
    tp = N_META + seq
    n_p, n_s = b * tp, db * dt
    n_tok = n_p + n_s
    n_pad = pl.cdiv(n_tok, PEER_TILE) * PEER_TILE
    bf = jnp.bfloat16
    past_len = page_table.shape[1] * PAGE_SIZE

    meta = jnp.broadcast_to(meta_tokens[None], (b, N_META, d))
    hp = jnp.concatenate([meta, x_prompt], axis=1).reshape(n_p, d)
    h = jnp.concatenate([hp, x_sample.reshape(n_s, d), jnp.zeros((n_pad - n_tok, d), jnp.float32)], axis=0)

    news_p, news_s = [], []
    for l in range(depth):
        lw = {'conv_a_w': conv_a_w[l], 'conv_a_b': conv_a_b[l], 'ln_a_g': ln_a_g[l], 'ln_a_b': ln_a_b[l],
              'conv_b_w': conv_b_w[l], 'pool_w': pool_w[l], 'pool_scale': pool_scale[l]}
        past = {'cache_k': cache_k, 'cache_v': cache_v, 'cache_kidx': cache_kidx, 'layer': l,
                'conv_a': state_conv_a[l], 'conv_b': state_conv_b[l], 'pool': state_pool[l],
                'page_table': page_table}
        n_mix = 5 * GROUP_W
        w_proj = jnp.concatenate([w_in[l][:, :n_mix], w_in[l][:, N_COLS - GROUP_W:], w_in[l][:, n_mix:N_COLS - GROUP_W]],
                                 axis=1).astype(bf)
        z = norm_matmul(h, norm_mix_g[l], w_proj)
        ys_p, new_p = mixer_block(z[:n_p].reshape(b, tp, N_COLS), lw, None, 0)
        ys_s, new_s = mixer_block(z[n_p:n_tok].reshape(db, dt, N_COLS), lw, past, past_len)
        ys = [jnp.concatenate([yp.reshape(n_p, GROUP_W), ysm.reshape(n_s, GROUP_W),
                               jnp.zeros((n_pad - n_tok, GROUP_W), jnp.float32)], axis=0)
              for yp, ysm in zip(ys_p, ys_s)]
        h = out_proj(h, ys, w_out[l].astype(bf))
        xn, k1, k2, gate = peer_route(h, norm_ffn_g[l], peer_wq[l].T.astype(bf),
                                      peer_subkeys[l].reshape(PEER_HEADS * 2, N_KEYS, PEER_HALF).astype(bf))
        h = peer_eval(xn, k1, k2, gate, peer_u[l].astype(bf), peer_v[l].astype(bf), h)
        news_p.append(new_p)
        news_s.append(new_s)

    y = rms_norm_rows(h, norm_final_g)
    y_prompt = y[:n_p].reshape(b, tp, d)[:, N_META:]
    y_sample = y[n_p:n_tok].reshape(db, dt, d)
    outs_p = [jnp.stack([n[i] for n in news_p]) for i in range(6)]
    outs_s = [jnp.stack([n[i] for n in news_s]) for i in range(6)]
    return (y_prompt, y_sample, *outs_p, *outs_s)
```

```python
import functools

import jax
import jax.numpy as jnp
from jax import lax
from jax.experimental import pallas as pl
from jax.experimental.pallas import tpu as pltpu

N_META = 16
GROUP_W = 256
CONV_A_WIDTH = 31
CONV_B_WIDTH = 3
N_HEADS = 4
HEAD_DIM = 64
N_IDX_HEADS = 8
IDX_DIM = 64
TOPK_MAX = 256
PAGE_SIZE = 128
POOL_WINDOWS = (2, 4, 8, 16)
POOL_GROUP = 64
POOL_CTX = 15
PEER_HEADS = 8
N_KEYS = 128
PEER_HALF = 128
PEER_TOPK = 16
N_ROUTES = PEER_HEADS * PEER_TOPK
RMS_EPS = 1e-6
LN_EPS = 1e-5
SPLIT_SIZES = (GROUP_W,) * 8 + (N_IDX_HEADS * IDX_DIM, IDX_DIM, N_IDX_HEADS, GROUP_W)
N_COLS = sum(SPLIT_SIZES)

Q_BLOCK = 128
KEY_CHUNK = 512
ROW_TILE = 256
ROUTE_TILE = 256
PEER_TILE = 512
HALF_KEYS = N_KEYS // 2
KEY_PAIRS_PER_STEP = 4
GATE_BUILD_UNROLL = 32
MIX_COLS = 6 * GROUP_W
MIX_TILE = 512
MIX_HALO = 32
SAMPLE_ROWS = 8
SAMPLE_GROUP = 2
PAGES_PER_CHUNK = 8
SAMPLE_CHUNK = PAGES_PER_CHUNK * PAGE_SIZE
INT_MIN = -2 ** 31
NEG_INF_KEY = -2139095041
VMEM_LIMIT_BYTES = 48 * 1024 * 1024
PEER_VMEM_LIMIT_BYTES = 56 * 1024 * 1024


def _compiler_params(vmem_limit_bytes=VMEM_LIMIT_BYTES, **kw):
    return pltpu.CompilerParams(vmem_limit_bytes=vmem_limit_bytes, **kw)


def _norm_matmul_kernel(x_ref, g_ref, w_ref, o_ref):
    x = x_ref[...]
    y = x * lax.rsqrt(jnp.mean(x * x, axis=-1, keepdims=True) + RMS_EPS) * g_ref[...]
    o_ref[...] = jnp.dot(y.astype(jnp.bfloat16), w_ref[...], preferred_element_type=jnp.float32)


def norm_matmul(x, g, w):
    n, d = x.shape
    c = w.shape[1]
    return pl.pallas_call(
        _norm_matmul_kernel,
        grid=(n // ROW_TILE,),
        in_specs=[pl.BlockSpec((ROW_TILE, d), lambda i: (i, 0)),
                  pl.BlockSpec((1, d), lambda i: (0, 0)),
                  pl.BlockSpec((d, c), lambda i: (0, 0))],
        out_specs=pl.BlockSpec((ROW_TILE, c), lambda i: (i, 0)),
        out_shape=jax.ShapeDtypeStruct((n, c), jnp.float32),
        compiler_params=_compiler_params(),
        name="norm_matmul",
    )(x, g.reshape(1, d), w)


def _out_proj_kernel(h_ref, ya_ref, yb_ref, yc_ref, yd_ref, w_ref, o_ref):
    acc = h_ref[...]
    for gi, y_ref in enumerate((ya_ref, yb_ref, yc_ref, yd_ref)):
        acc += jnp.dot(y_ref[...].astype(jnp.bfloat16), w_ref[gi * GROUP_W:(gi + 1) * GROUP_W, :],
                       preferred_element_type=jnp.float32)
    o_ref[...] = acc


def out_proj(h, ys, w):
    n, d = h.shape
    y_spec = pl.BlockSpec((ROW_TILE, GROUP_W), lambda i: (i, 0))
    return pl.pallas_call(
        _out_proj_kernel,
        grid=(n // ROW_TILE,),
        in_specs=[pl.BlockSpec((ROW_TILE, d), lambda i: (i, 0)), y_spec, y_spec, y_spec, y_spec,
                  pl.BlockSpec(w.shape, lambda i: (0, 0))],
        out_specs=pl.BlockSpec((ROW_TILE, d), lambda i: (i, 0)),
        out_shape=jax.ShapeDtypeStruct((n, d), jnp.float32),
        compiler_params=_compiler_params(),
        name="out_proj",
    )(h, *ys, w)


def _rms_norm_kernel(x_ref, g_ref, o_ref):
    x = x_ref[...]
    o_ref[...] = x * lax.rsqrt(jnp.mean(x * x, axis=-1, keepdims=True) + RMS_EPS) * g_ref[...]


def rms_norm_rows(x, g):
    n, d = x.shape
    return pl.pallas_call(
        _rms_norm_kernel,
        grid=(n // ROW_TILE,),
        in_specs=[pl.BlockSpec((ROW_TILE, d), lambda i: (i, 0)), pl.BlockSpec((1, d), lambda i: (0, 0))],
        out_specs=pl.BlockSpec((ROW_TILE, d), lambda i: (i, 0)),
        out_shape=jax.ShapeDtypeStruct((n, d), jnp.float32),
        compiler_params=_compiler_params(),
        name="final_norm",
    )(x, g.reshape(1, d))


def _sortable_key(x):
    x = jnp.where(x == 0.0, 0.0, x)
    b = pltpu.bitcast(x, jnp.int32)
    return b ^ ((b >> 31) & 0x7FFFFFFF)


def _col_sum(x):
    return jnp.sum(x.reshape(x.shape[0] // 8, 8, x.shape[1]), axis=0)


def _dsa_prompt_kernel(qt_ref, qit_ref, wt_ref, kidx_ref, k_ref, vt_ref, o_ref, key_ref, *, topk):
    i = pl.program_id(1)
    n_chunks = pl.cdiv((i + 1) * Q_BLOCK, KEY_CHUNK)
    q_pos = i * Q_BLOCK + lax.broadcasted_iota(jnp.int32, (KEY_CHUNK, Q_BLOCK), 1)
    key_row = lax.broadcasted_iota(jnp.int32, (KEY_CHUNK, Q_BLOCK), 0)
    qit = qit_ref[...]
    wt = wt_ref[...]

    def score_chunk(c, carry):
        off = pl.multiple_of(c * KEY_CHUNK, KEY_CHUNK)
        s = jnp.dot(kidx_ref[pl.ds(off, KEY_CHUNK), :], qit, preferred_element_type=jnp.float32)
        s = jnp.maximum(s, 0.0)
        score = wt[0:1] * s[:, 0:Q_BLOCK]
        for h in range(1, N_IDX_HEADS):
            score = score + wt[h:h + 1] * s[:, h * Q_BLOCK:(h + 1) * Q_BLOCK]
        visible = (off + key_row) <= q_pos
        key_ref[c] = jnp.where(visible, _sortable_key(score), NEG_INF_KEY)
        return carry

    lax.fori_loop(0, n_chunks, score_chunk, 0)

    def count(pred):
        def body(c, acc):
            p = pred(key_ref[c]).astype(jnp.float32)
            return acc + p[:KEY_CHUNK // 2] + p[KEY_CHUNK // 2:]
        acc = lax.fori_loop(0, n_chunks, body, jnp.zeros((KEY_CHUNK // 2, Q_BLOCK), jnp.float32))
        return jnp.sum(acc, axis=0, keepdims=True)

    kf = float(topk)
    t0 = jnp.where(count(lambda k: k >= 0) >= kf, 0, INT_MIN).astype(jnp.int32)

    def search(it, t):
        cand = t + (jnp.int32(1) << (30 - it))
        return jnp.where(count(lambda k: k >= cand) >= kf, cand, t)

    thr = lax.fori_loop(0, 31, search, t0)
    n_take_eq = kf - count(lambda k: k > thr)

    d_head = lax.broadcasted_iota(jnp.int32, (GROUP_W, Q_BLOCK), 0) // HEAD_DIM
    qt = qt_ref[...]
    q_heads = jnp.concatenate([jnp.where(d_head == h, qt, jnp.zeros_like(qt)) for h in range(N_HEADS)], axis=1)
    tri = (lax.broadcasted_iota(jnp.int32, (KEY_CHUNK, KEY_CHUNK), 1)
           < lax.broadcasted_iota(jnp.int32, (KEY_CHUNK, KEY_CHUNK), 0)).astype(jnp.bfloat16)

    def attend_chunk(c, carry):
        eq_before, m, l, acc = carry
        off = pl.multiple_of(c * KEY_CHUNK, KEY_CHUNK)
        keys = key_ref[c]
        eq = keys == thr
        eq_excl = jnp.dot(tri, eq.astype(jnp.bfloat16), preferred_element_type=jnp.float32)
        sel = (keys > thr) | (eq & ((eq_before + eq_excl) < n_take_eq))
        sel = sel & (keys != NEG_INF_KEY)
        lg_all = jnp.dot(k_ref[pl.ds(off, KEY_CHUNK), :], q_heads,
                         preferred_element_type=jnp.float32) * (HEAD_DIM ** -0.5)
        vt = vt_ref[c]
        m_rows, l_rows, acc_rows = [], [], []
        for h in range(N_HEADS):
            lg = jnp.where(sel, lg_all[:, h * Q_BLOCK:(h + 1) * Q_BLOCK], -jnp.inf)
            m_old = m[h:h + 1]
            m_new = jnp.maximum(m_old, jnp.max(lg, axis=0, keepdims=True))
            m_safe = jnp.where(m_new == -jnp.inf, 0.0, m_new)
            p = jnp.exp(lg - m_safe)
            alpha = jnp.exp(m_old - m_safe)
            l_rows.append(alpha * l[h:h + 1] + jnp.sum(_col_sum(p), axis=0, keepdims=True))
            m_rows.append(m_new)
            pv = jnp.dot(vt[h * HEAD_DIM:(h + 1) * HEAD_DIM], p.astype(jnp.bfloat16),
                         preferred_element_type=jnp.float32)
            acc_rows.append(alpha * acc[h * HEAD_DIM:(h + 1) * HEAD_DIM] + pv)
        eq_after = eq_before + jnp.sum(_col_sum(eq.astype(jnp.float32)), axis=0, keepdims=True)
        return (eq_after, jnp.concatenate(m_rows, axis=0), jnp.concatenate(l_rows, axis=0),
                jnp.concatenate(acc_rows, axis=0))

    init = (jnp.zeros((1, Q_BLOCK), jnp.float32), jnp.full((N_HEADS, Q_BLOCK), -jnp.inf, jnp.float32),
            jnp.zeros((N_HEADS, Q_BLOCK), jnp.float32), jnp.zeros((GROUP_W, Q_BLOCK), jnp.float32))
    _, _, l, acc = lax.fori_loop(0, n_chunks, attend_chunk, init)
    out_t = jnp.concatenate([acc[h * HEAD_DIM:(h + 1) * HEAD_DIM] / l[h:h + 1] for h in range(N_HEADS)], axis=0)
    o_ref[...] = out_t.T


def dsa_prompt(q, k, v, q_idx, k_idx, w_idx, topk):
    b, t, _ = q.shape
    n_blk = pl.cdiv(t, Q_BLOCK)
    tq = n_blk * Q_BLOCK
    tk = pl.cdiv(tq, KEY_CHUNK) * KEY_CHUNK
    n_ch = tk // KEY_CHUNK
    bf = jnp.bfloat16
    pad_q = lambda a: jnp.pad(a, [(0, 0), (0, tq - t), (0, 0)])
    pad_k = lambda a: jnp.pad(a, [(0, 0), (0, tk - t), (0, 0)])
    qt = pad_q(q).astype(bf).reshape(b, n_blk, Q_BLOCK, GROUP_W).transpose(0, 1, 3, 2)
    qit = (pad_q(q_idx).astype(bf).reshape(b, n_blk, Q_BLOCK, N_IDX_HEADS, IDX_DIM)
           .transpose(0, 1, 4, 3, 2).reshape(b, n_blk, IDX_DIM, N_IDX_HEADS * Q_BLOCK))
    wt = pad_q(w_idx).reshape(b, n_blk, Q_BLOCK, N_IDX_HEADS).transpose(0, 1, 3, 2)
    vt = pad_k(v).astype(bf).reshape(b, n_ch, KEY_CHUNK, GROUP_W).transpose(0, 1, 3, 2)
    out = pl.pallas_call(
        functools.partial(_dsa_prompt_kernel, topk=topk),
        grid=(b, n_blk),
        in_specs=[
            pl.BlockSpec((None, None, GROUP_W, Q_BLOCK), lambda bi, i: (bi, i, 0, 0)),
            pl.BlockSpec((None, None, IDX_DIM, N_IDX_HEADS * Q_BLOCK), lambda bi, i: (bi, i, 0, 0)),
            pl.BlockSpec((None, None, N_IDX_HEADS, Q_BLOCK), lambda bi, i: (bi, i, 0, 0)),
            pl.BlockSpec((None, tk, IDX_DIM), lambda bi, i: (bi, 0, 0)),
            pl.BlockSpec((None, tk, GROUP_W), lambda bi, i: (bi, 0, 0)),
            pl.BlockSpec((None, n_ch, GROUP_W, KEY_CHUNK), lambda bi, i: (bi, 0, 0, 0)),
        ],
        out_specs=pl.BlockSpec((None, Q_BLOCK, GROUP_W), lambda bi, i: (bi, i, 0)),
        out_shape=jax.ShapeDtypeStruct((b, tq, GROUP_W), jnp.float32),
        scratch_shapes=[pltpu.VMEM((n_ch, KEY_CHUNK, Q_BLOCK), jnp.int32)],
        compiler_params=_compiler_params(),
        name="dsa_prompt",
    )(qt, qit, wt, pad_k(k_idx).astype(bf), pad_k(k).astype(bf), vt)
    return out[:, :t]


def _dsa_sample_kernel(pt_ref, q_ref, qi_ref, w_ref, kin_ref, kn_ref, vn_ref, kidx_hbm, k_hbm, v_hbm, o_ref,
                       kidx_buf, k_buf, v_buf, key_ref, m_ref, l_ref, acc_ref, sems, *, layer, n_pages, topk):
    step = pl.program_id(0)
    n_chunks = key_ref.shape[0]
    n_buf_pages = kidx_buf.shape[1]
    rows = SAMPLE_ROWS
    group = SAMPLE_GROUP

    def page_copy(s, g, p, which):
        src, dst = ((kidx_hbm, kidx_buf), (k_hbm, k_buf), (v_hbm, v_buf))[which]
        half = s % 2
        return pltpu.make_async_copy(src.at[layer, pt_ref[s * group + g, p]], dst.at[half * group + g, p],
                                     sems.at[half, which])

    def start_pages(s):
        def body(p, carry):
            for g in range(group):
                for which in range(3):
                    page_copy(s, g, p, which).start()
            return carry
        lax.fori_loop(0, n_pages, body, 0)

    @pl.when(step == 0)
    def _():
        start_pages(step)

    @pl.when(step + 1 < pl.num_programs(0))
    def _():
        start_pages(step + 1)

    base = (step % 2) * group

    for g in range(group):
        for p in range(n_pages, n_buf_pages):
            for buf, new_ref in ((kidx_buf, kin_ref), (k_buf, kn_ref), (v_buf, vn_ref)):
                buf[base + g, p] = (new_ref[g].astype(buf.dtype) if p == n_pages
                                    else jnp.zeros(buf.shape[2:], buf.dtype))

    def wait_pages(which):
        def body(p, carry):
            for g in range(group):
                page_copy(step, g, p, which).wait()
            return carry
        lax.fori_loop(0, n_pages, body, 0)

    wait_pages(0)

    past_len = n_pages * PAGE_SIZE
    q_row = lax.broadcasted_iota(jnp.int32, (group * rows, SAMPLE_CHUNK), 0) % rows
    lane = lax.broadcasted_iota(jnp.int32, (group * rows, SAMPLE_CHUNK), 1)
    qi = [qi_ref[g].reshape(N_IDX_HEADS * rows, IDX_DIM) for g in range(group)]
    w_cols = [[jnp.broadcast_to(w_ref[g][:, h:h + 1], (rows, SAMPLE_CHUNK)) for h in range(N_IDX_HEADS)]
              for g in range(group)]

    def score_chunk(c, carry):
        p0 = pl.multiple_of(c * PAGES_PER_CHUNK, PAGES_PER_CHUNK)
        scores = []
        for g in range(group):
            kc = kidx_buf[base + g, pl.ds(p0, PAGES_PER_CHUNK)].reshape(SAMPLE_CHUNK, IDX_DIM)
            s = lax.dot_general(qi[g], kc, (((1,), (1,)), ((), ())), preferred_element_type=jnp.float32)
            s = jnp.maximum(s, 0.0)
            score = w_cols[g][0] * s[0:rows]
            for h in range(1, N_IDX_HEADS):
                score = score + w_cols[g][h] * s[h * rows:(h + 1) * rows]
            scores.append(score)
        visible = (c * SAMPLE_CHUNK + lane) <= (past_len + q_row)
        key_ref[c] = jnp.where(visible, _sortable_key(jnp.concatenate(scores, axis=0)), NEG_INF_KEY)
        return carry

    lax.fori_loop(0, n_chunks, score_chunk, 0)

    def count(pred):
        def body(c, acc):
            return acc + pred(key_ref[c]).astype(jnp.float32)
        acc = lax.fori_loop(0, n_chunks, body, jnp.zeros((group * rows, SAMPLE_CHUNK), jnp.float32))
        return jnp.sum(acc, axis=-1, keepdims=True)

    kf = float(topk)
    t0 = jnp.where(count(lambda k: k >= 0) >= kf, 0, INT_MIN).astype(jnp.int32)

    def search(it, t):
        cand = t + (jnp.int32(1) << (30 - it))
        return jnp.where(count(lambda k: k >= cand) >= kf, cand, t)

    thr = lax.fori_loop(0, 31, search, t0)
    n_take_eq = kf - count(lambda k: k > thr)

    wait_pages(1)
    wait_pages(2)

    head_of_lane = lax.broadcasted_iota(jnp.int32, (rows, GROUP_W), 1) // HEAD_DIM
    q_heads = [jnp.concatenate([jnp.where(head_of_lane == h, q_ref[g], jnp.zeros((rows, GROUP_W), q_ref.dtype))
                                for h in range(N_HEADS)], axis=0) for g in range(group)]
    tri = (lax.broadcasted_iota(jnp.int32, (SAMPLE_CHUNK, SAMPLE_CHUNK), 0)
           < lax.broadcasted_iota(jnp.int32, (SAMPLE_CHUNK, SAMPLE_CHUNK), 1)).astype(jnp.bfloat16)

    m_ref[...] = jnp.full(m_ref.shape, -jnp.inf, jnp.float32)
    l_ref[...] = jnp.zeros(l_ref.shape, jnp.float32)
    acc_ref[...] = jnp.zeros(acc_ref.shape, jnp.float32)
    hr = N_HEADS * rows

    def attend_chunk(c, eq_before):
        p0 = pl.multiple_of(c * PAGES_PER_CHUNK, PAGES_PER_CHUNK)
        keys = key_ref[c]
        eq = keys == thr
        eq_excl = jnp.dot(eq.astype(jnp.bfloat16), tri, preferred_element_type=jnp.float32)
        sel = (keys > thr) | (eq & ((eq_before + eq_excl) < n_take_eq))
        sel = sel & (keys != NEG_INF_KEY)
        lgs, vcs = [], []
        for g in range(group):
            kc = k_buf[base + g, pl.ds(p0, PAGES_PER_CHUNK)].reshape(SAMPLE_CHUNK, GROUP_W)
            vcs.append(v_buf[base + g, pl.ds(p0, PAGES_PER_CHUNK)].reshape(SAMPLE_CHUNK, GROUP_W))
            lg = lax.dot_general(q_heads[g], kc, (((1,), (1,)), ((), ())),
                                 preferred_element_type=jnp.float32) * (HEAD_DIM ** -0.5)
            sel_g = jnp.concatenate([sel[g * rows:(g + 1) * rows]] * N_HEADS, axis=0)
            lgs.append(jnp.where(sel_g, lg, -jnp.inf))
        lg = jnp.concatenate(lgs, axis=0)
        m_old = m_ref[...]
        m_new = jnp.maximum(m_old, jnp.max(lg, axis=-1, keepdims=True))
        m_safe = jnp.where(m_new == -jnp.inf, 0.0, m_new)
        p = jnp.exp(lg - m_safe)
        alpha = jnp.exp(m_old - m_safe)
        l_ref[...] = alpha * l_ref[...] + jnp.sum(p, axis=-1, keepdims=True)
        m_ref[...] = m_new
        pb = p.astype(jnp.bfloat16)
        pv = jnp.concatenate([jnp.dot(pb[g * hr:(g + 1) * hr], vcs[g], preferred_element_type=jnp.float32)
                              for g in range(group)], axis=0)
        acc_ref[...] = alpha * acc_ref[...] + pv
        return eq_before + jnp.sum(eq.astype(jnp.float32), axis=-1, keepdims=True)

    lax.fori_loop(0, n_chunks, attend_chunk, jnp.zeros((group * rows, 1), jnp.float32))

    res = acc_ref[...] / l_ref[...]
    for g in range(group):
        out = jnp.zeros((rows, GROUP_W), jnp.float32)
        for h in range(N_HEADS):
            out = jnp.where(head_of_lane == h, res[g * hr + h * rows:g * hr + (h + 1) * rows], out)
        o_ref[g] = out


def dsa_sample(q, k, v, q_idx, k_idx, w_idx, cache_k, cache_v, cache_kidx, page_table, layer):
    db, t, _ = q.shape
    n_pages = page_table.shape[1]
    depth, n_pool = cache_k.shape[:2]
    topk = min(TOPK_MAX, (n_pages * PAGE_SIZE + t) // 4)
    rows = SAMPLE_ROWS
    group = SAMPLE_GROUP
    assert db % group == 0
    n_chunks = pl.cdiv(n_pages + 1, PAGES_PER_CHUNK)
    n_buf_pages = n_chunks * PAGES_PER_CHUNK
    bf = jnp.bfloat16
    pad_rows = lambda a, r: jnp.pad(a, [(0, 0), (0, r - t), (0, 0)])
    qi = pad_rows(q_idx, rows).astype(bf).reshape(db, rows, N_IDX_HEADS, IDX_DIM).transpose(0, 2, 1, 3)
    per_g = lambda *shape: pl.BlockSpec((group,) + shape, lambda s, pt: (s,) + (0,) * len(shape))
    hbm = pl.BlockSpec(memory_space=pl.ANY)
    grid_spec = pltpu.PrefetchScalarGridSpec(
        num_scalar_prefetch=1,
        grid=(db // group,),
        in_specs=[per_g(rows, GROUP_W), per_g(N_IDX_HEADS, rows, IDX_DIM), per_g(rows, N_IDX_HEADS),
                  per_g(PAGE_SIZE, IDX_DIM), per_g(PAGE_SIZE, GROUP_W), per_g(PAGE_SIZE, GROUP_W),
                  hbm, hbm, hbm],
        out_specs=per_g(rows, GROUP_W),
        scratch_shapes=[
            pltpu.VMEM((2 * group, n_buf_pages, PAGE_SIZE, IDX_DIM), bf),
            pltpu.VMEM((2 * group, n_buf_pages, PAGE_SIZE, GROUP_W), bf),
            pltpu.VMEM((2 * group, n_buf_pages, PAGE_SIZE, GROUP_W), bf),
            pltpu.VMEM((n_chunks, group * rows, SAMPLE_CHUNK), jnp.int32),
            pltpu.VMEM((group * N_HEADS * rows, 1), jnp.float32),
            pltpu.VMEM((group * N_HEADS * rows, 1), jnp.float32),
            pltpu.VMEM((group * N_HEADS * rows, GROUP_W), jnp.float32),
            pltpu.SemaphoreType.DMA((2, 3)),
        ],
    )
    out = pl.pallas_call(
        functools.partial(_dsa_sample_kernel, layer=layer, n_pages=n_pages, topk=topk),
        grid_spec=grid_spec,
        out_shape=jax.ShapeDtypeStruct((db, rows, GROUP_W), jnp.float32),
        compiler_params=_compiler_params(PEER_VMEM_LIMIT_BYTES, dimension_semantics=("arbitrary",)),
        name="dsa_sample",
    )(page_table, pad_rows(q, rows).astype(bf), qi, pad_rows(w_idx, rows),
      pad_rows(k_idx, PAGE_SIZE), pad_rows(k, PAGE_SIZE), pad_rows(v, PAGE_SIZE),
      cache_kidx.astype(bf), cache_k.astype(bf).reshape(depth, n_pool, PAGE_SIZE, GROUP_W),
      cache_v.astype(bf).reshape(depth, n_pool, PAGE_SIZE, GROUP_W))
    return out[:, :t]


def _mixer_streams(z):
    a_val, a_gate, b_gate, c_gate, b_in, pool_in = [z[:, g * GROUP_W:(g + 1) * GROUP_W] for g in range(6)]
    return a_val * jax.nn.sigmoid(a_gate), c_gate * b_in, pool_in, b_gate


def _mixers_kernel(z_ref, ctx_ref, wa_ref, ba_ref, lng_ref, lnb_ref, wb_ref, wp_ref, ps_ref,
                   ya_ref, yb_ref, yd_ref, ta_ref, tb_ref, tp_ref, ext_ref, *, ctx_from_z, start_pos, n_valid_last):
    i = pl.program_id(1)
    tt = z_ref.shape[0]
    u, g_in, p_in, b_gate = _mixer_streams(z_ref[...])
    if ctx_from_z:
        hu, hg, hp, _ = _mixer_streams(ctx_ref[...])
        hu, hg, hp = [jnp.where(i > 0, a, 0.0) for a in (hu, hg, hp)]
    else:
        hu, hg, hp = [ctx_ref[:, g * GROUP_W:(g + 1) * GROUP_W] for g in range(3)]
    for s, (head, body) in enumerate(((hu, u), (hg, g_in), (hp, p_in))):
        ext_ref[s, 0:MIX_HALO] = head
        ext_ref[s, MIX_HALO:MIX_HALO + tt] = body

    def rows(s, back):
        return ext_ref[s, pl.ds(MIX_HALO - back, tt), :]

    acc = jnp.zeros((tt, GROUP_W), jnp.float32)
    for j in range(CONV_A_WIDTH):
        acc = acc + wa_ref[j:j + 1, :] * rows(0, CONV_A_WIDTH - 1 - j)
    acc = acc + ba_ref[...]
    mu = jnp.mean(acc, axis=-1, keepdims=True)
    var = jnp.mean(jnp.square(acc - mu), axis=-1, keepdims=True)
    ln = (acc - mu) * lax.rsqrt(var + LN_EPS) * lng_ref[...] + lnb_ref[...]
    ya_ref[...] = ln * jax.nn.sigmoid(ln)

    cb = jnp.zeros((tt, GROUP_W), jnp.float32)
    for j in range(CONV_B_WIDTH):
        cb = cb + wb_ref[j:j + 1, :] * rows(1, CONV_B_WIDTH - 1 - j)
    yb_ref[...] = b_gate * cb

    lane_group = lax.broadcasted_iota(jnp.int32, (tt, GROUP_W), 1) // POOL_GROUP
    pos1 = start_pos + i * tt + lax.broadcasted_iota(jnp.int32, (tt, GROUP_W), 0) + 1
    run = p_in
    mean = jnp.zeros((tt, GROUP_W), jnp.float32)
    back = 1
    for gi, w in enumerate(POOL_WINDOWS):
        while back < w:
            run = run + rows(2, back)
            back += 1
        cnt = jnp.minimum(pos1, w).astype(jnp.float32)
        mean = jnp.where(lane_group == gi, run / cnt, mean)
    pooled = (mean - p_in).astype(jnp.bfloat16)
    yd_ref[...] = jnp.dot(pooled, wp_ref[...], preferred_element_type=jnp.float32) * ps_ref[...]

    @pl.when(i == pl.num_programs(1) - 1)
    def _():
        end = MIX_HALO + n_valid_last
        ta_ref[...] = ext_ref[0, end - (CONV_A_WIDTH - 1):end, :]
        tb_ref[...] = ext_ref[1, end - (CONV_B_WIDTH - 1):end, :]
        tp_ref[...] = ext_ref[2, end - POOL_CTX:end, :]


def mixers(z, ctx, lw, start_pos):
    b, t, _ = z.shape
    tt = min(MIX_TILE, t)
    n_tiles = pl.cdiv(t, tt)
    ctx_from_z = ctx is None
    if ctx_from_z:
        per_tile = tt // MIX_HALO
        ctx_arr = z
        ctx_spec = pl.BlockSpec((None, MIX_HALO, MIX_COLS), lambda bi, i: (bi, jnp.maximum(i * per_tile - 1, 0), 0))
    else:
        ctx_arr = ctx
        ctx_spec = pl.BlockSpec((None, MIX_HALO, 3 * GROUP_W), lambda bi, i: (bi, 0, 0))
    wp = jnp.zeros((GROUP_W, GROUP_W), jnp.float32)
    for gi in range(len(POOL_WINDOWS)):
        wp = wp.at[gi * POOL_GROUP:(gi + 1) * POOL_GROUP, gi * POOL_GROUP:(gi + 1) * POOL_GROUP].set(lw['pool_w'][gi])
    row = lambda a: a.reshape(1, GROUP_W)
    const = lambda shape: pl.BlockSpec(shape, lambda bi, i: (0,) * len(shape))
    y_spec = pl.BlockSpec((None, tt, GROUP_W), lambda bi, i: (bi, i, 0))
    tail = lambda r: pl.BlockSpec((None, r, GROUP_W), lambda bi, i: (bi, 0, 0))
    f32 = jnp.float32
    return pl.pallas_call(
        functools.partial(_mixers_kernel, ctx_from_z=ctx_from_z, start_pos=start_pos,
                          n_valid_last=t - (n_tiles - 1) * tt),
        grid=(b, n_tiles),
        in_specs=[pl.BlockSpec((None, tt, MIX_COLS), lambda bi, i: (bi, i, 0)), ctx_spec,
                  const((CONV_A_WIDTH, GROUP_W)), const((1, GROUP_W)), const((1, GROUP_W)), const((1, GROUP_W)),
                  const((CONV_B_WIDTH, GROUP_W)), const((GROUP_W, GROUP_W)), const((1, GROUP_W))],
        out_specs=[y_spec, y_spec, y_spec, tail(CONV_A_WIDTH - 1), tail(CONV_B_WIDTH - 1), tail(POOL_CTX)],
        out_shape=[jax.ShapeDtypeStruct((b, t, GROUP_W), f32)] * 3
                  + [jax.ShapeDtypeStruct((b, r, GROUP_W), f32) for r in (CONV_A_WIDTH - 1, CONV_B_WIDTH - 1, POOL_CTX)],
        scratch_shapes=[pltpu.VMEM((3, MIX_HALO + tt, GROUP_W), f32)],
        compiler_params=_compiler_params(),
        name="mixers",
    )(z, ctx_arr, lw['conv_a_w'], row(lw['conv_a_b']), row(lw['ln_a_g']), row(lw['ln_a_b']),
      lw['conv_b_w'], wp.astype(jnp.bfloat16), row(lw['pool_scale']))


def mixer_block(z, lw, past, start_pos):
    b, t, _ = z.shape
    parts, start = [], MIX_COLS
    for size in (GROUP_W, GROUP_W, GROUP_W, N_IDX_HEADS * IDX_DIM, IDX_DIM, N_IDX_HEADS):
        parts.append(z[..., start:start + size])
        start += size
    q, k, v, q_idx, k_idx, w_idx = parts
    if past is None:
        ctx = None
        y_c = dsa_prompt(q, k, v, q_idx, k_idx, w_idx, min(TOPK_MAX, t // 4))
    else:
        front = lambda a: jnp.pad(a, [(0, 0), (MIX_HALO - a.shape[1], 0), (0, 0)])
        ctx = jnp.concatenate([front(past['conv_a']), front(past['conv_b']), front(past['pool'])], axis=-1)
        y_c = dsa_sample(q, k, v, q_idx, k_idx, w_idx, past['cache_k'], past['cache_v'], past['cache_kidx'],
                         past['page_table'], past['layer'])
    y_a, y_b, y_d, tail_a, tail_b, tail_p = mixers(z, ctx, lw, start_pos)
    new = (k.reshape(b, t, N_HEADS, HEAD_DIM), v.reshape(b, t, N_HEADS, HEAD_DIM), k_idx, tail_a, tail_b, tail_p)
    return (y_a, y_b, y_c, y_d), new


def _top_rows(s, k):
    r = s.shape[0]
    row = lax.broadcasted_iota(jnp.int32, s.shape, 0)
    vals, idxs = [], []
    for _ in range(k):
        m = jnp.max(s, axis=0, keepdims=True)
        idx = jnp.min(jnp.where(s == m, row, r), axis=0, keepdims=True)
        s = jnp.where(row == idx, -jnp.inf, s)
        vals.append(m)
        idxs.append(idx)
    return jnp.concatenate(vals, axis=0), jnp.concatenate(idxs, axis=0)


def _pick_rows(table, which):
    row = lax.broadcasted_iota(jnp.int32, table.shape, 0)
    out = []
    for j in range(which.shape[0]):
        out.append(jnp.sum(jnp.where(row == which[j:j + 1], table, 0), axis=0, keepdims=True))
    return jnp.concatenate(out, axis=0)


CAND_MID_B = 8
CAND_MID_START = PEER_TOPK
CAND_TAIL_START = CAND_MID_START + 7 * CAND_MID_B


def _cand_a(r):
    return jnp.where(r < CAND_MID_START, 0,
                     jnp.where(r < CAND_TAIL_START, (r - CAND_MID_START) // CAND_MID_B + 1,
                               r - CAND_TAIL_START + CAND_MID_B))


def _cand_b(r):
    return jnp.where(r < CAND_MID_START, r, jnp.where(r < CAND_TAIL_START, (r - CAND_MID_START) % CAND_MID_B, 0))


def _peer_route_kernel(h_ref, g_ref, wqt_ref, sk_ref, xn_ref, k1_ref, k2_ref, gate_ref,
                       qt_ref, k1t_ref, k2t_ref, gt_ref):
    x = h_ref[...]
    xn = (x * lax.rsqrt(jnp.mean(x * x, axis=-1, keepdims=True) + RMS_EPS) * g_ref[...]).astype(jnp.bfloat16)
    xn_ref[...] = xn
    qt_ref[...] = lax.dot_general(wqt_ref[...], xn, (((1,), (1,)), ((), ())),
                                  preferred_element_type=jnp.float32).astype(jnp.bfloat16)

    def head(h, carry):
        r0 = pl.multiple_of(h * 2 * PEER_HALF, 2 * PEER_HALF)
        s1 = jnp.dot(sk_ref[2 * h], qt_ref[pl.ds(r0, PEER_HALF), :], preferred_element_type=jnp.float32)
        s2 = jnp.dot(sk_ref[2 * h + 1], qt_ref[pl.ds(r0 + PEER_HALF, PEER_HALF), :],
                     preferred_element_type=jnp.float32)
        v1, i1 = _top_rows(s1, PEER_TOPK)
        v2, i2 = _top_rows(s2, PEER_TOPK)
        cand = jnp.concatenate([v1[0:1] + v2] + [v1[a:a + 1] + v2[0:CAND_MID_B] for a in range(1, CAND_MID_B)]
                               + [v1[CAND_MID_B:PEER_TOPK] + v2[0:1]], axis=0)
        r = lax.broadcasted_iota(jnp.int32, cand.shape, 0)
        cand = jnp.where((_cand_a(r) + 1) * (_cand_b(r) + 1) <= PEER_TOPK, cand, -jnp.inf)
        best, ridx = _top_rows(cand, PEER_TOPK)
        e = jnp.exp(best - best[0:1])
        o0 = pl.multiple_of(h * PEER_TOPK, PEER_TOPK)
        k1t_ref[pl.ds(o0, PEER_TOPK), :] = _pick_rows(i1, _cand_a(ridx))
        k2t_ref[pl.ds(o0, PEER_TOPK), :] = _pick_rows(i2, _cand_b(ridx))
        gt_ref[pl.ds(o0, PEER_TOPK), :] = e / jnp.sum(e, axis=0, keepdims=True)
        return carry

    lax.fori_loop(0, PEER_HEADS, head, 0)
    k1_ref[...] = k1t_ref[...].T
    k2_ref[...] = k2t_ref[...].T
    gate_ref[...] = gt_ref[...].T


def peer_route(h, g, wqt, sk):
    n, d = h.shape
    tm = ROUTE_TILE
    row_spec = pl.BlockSpec((tm, N_ROUTES), lambda i: (i, 0))
    return pl.pallas_call(
        _peer_route_kernel,
        grid=(n // tm,),
        in_specs=[pl.BlockSpec((tm, d), lambda i: (i, 0)),
                  pl.BlockSpec((1, d), lambda i: (0, 0)),
                  pl.BlockSpec(wqt.shape, lambda i: (0, 0)),
                  pl.BlockSpec(sk.shape, lambda i: (0, 0, 0))],
        out_specs=[pl.BlockSpec((tm, d), lambda i: (i, 0)), row_spec, row_spec, row_spec],
        out_shape=[jax.ShapeDtypeStruct((n, d), jnp.bfloat16),
                   jax.ShapeDtypeStruct((n, N_ROUTES), jnp.int32),
                   jax.ShapeDtypeStruct((n, N_ROUTES), jnp.int32),
                   jax.ShapeDtypeStruct((n, N_ROUTES), jnp.float32)],
        scratch_shapes=[pltpu.VMEM((PEER_HEADS * 2 * PEER_HALF, tm), jnp.bfloat16),
                        pltpu.VMEM((N_ROUTES, tm), jnp.int32),
                        pltpu.VMEM((N_ROUTES, tm), jnp.int32),
                        pltpu.VMEM((N_ROUTES, tm), jnp.float32)],
        compiler_params=_compiler_params(),
        name="peer_route",
    )(h, g.reshape(1, d), wqt, sk)


def _peer_eval_kernel(xn_ref, k1_ref, k2_ref, gate_ref, u_ref, v_ref, h_ref, o_ref, w_ref):
    j = pl.program_id(1)
    tn = xn_ref.shape[0]

    @pl.when(j == 0)
    def _():
        sub = lax.broadcasted_iota(jnp.int32, (N_KEYS, N_ROUTES), 0)

        def token(n, carry):
            k1 = jnp.broadcast_to(k1_ref[pl.ds(n, 1), :], (N_KEYS, N_ROUTES))
            k2 = jnp.broadcast_to(k2_ref[pl.ds(n, 1), :], (N_KEYS, N_ROUTES))
            gt = jnp.broadcast_to(gate_ref[pl.ds(n, 1), :], (N_KEYS, N_ROUTES))
            left = jnp.where(sub == k1, gt, 0.0).astype(jnp.bfloat16)
            right = (sub == k2).astype(jnp.bfloat16)
            tile = lax.dot_general(left, right, (((1,), (1,)), ((), ())), preferred_element_type=jnp.float32)
            tile = tile.astype(jnp.bfloat16).astype(jnp.float32)
            lo = lax.shift_right_logical(pltpu.bitcast(tile[:HALF_KEYS], jnp.int32), 16)
            hi = pltpu.bitcast(tile[HALF_KEYS:], jnp.int32) & jnp.int32(-65536)
            w_ref[pl.ds(pl.multiple_of(n * HALF_KEYS, HALF_KEYS), HALF_KEYS), :] = hi | lo
            return carry

        lax.fori_loop(0, tn, token, 0, unroll=GATE_BUILD_UNROLL)
        o_ref[...] = h_ref[...]

    eb = KEY_PAIRS_PER_STEP * N_KEYS
    u_blk = u_ref[...].reshape(2 * eb, u_ref.shape[-1])
    v_blk = v_ref[...].reshape(2 * eb, v_ref.shape[-1])
    act = lax.dot_general(xn_ref[...], u_blk, (((1,), (1,)), ((), ())), preferred_element_type=jnp.float32)
    words = [w_ref[pl.ds(j * KEY_PAIRS_PER_STEP + c, tn, stride=HALF_KEYS), :] for c in range(KEY_PAIRS_PER_STEP)]
    wgt = jnp.concatenate([pltpu.bitcast(wd << 16, jnp.float32) for wd in words]
                          + [pltpu.bitcast(wd & jnp.int32(-65536), jnp.float32) for wd in words], axis=1)
    coef = (wgt * jax.nn.gelu(act)).astype(jnp.bfloat16)
    o_ref[...] += jnp.dot(coef, v_blk, preferred_element_type=jnp.float32)


def peer_eval(xn, k1, k2, gate, u, v, h):
    n, d = xn.shape
    tn = PEER_TILE
    eb = KEY_PAIRS_PER_STEP * N_KEYS
    n_exp = u.shape[0]
    tok = lambda i, j: (i, 0)
    halves = pl.BlockSpec((2, eb, d), lambda i, j: (0, j, 0))
    return pl.pallas_call(
        _peer_eval_kernel,
        grid=(n // tn, n_exp // (2 * eb)),
        in_specs=[pl.BlockSpec((tn, d), tok),
                  pl.BlockSpec((tn, N_ROUTES), tok), pl.BlockSpec((tn, N_ROUTES), tok),
                  pl.BlockSpec((tn, N_ROUTES), tok),
                  halves, halves,
                  pl.BlockSpec((tn, d), tok)],
        out_specs=pl.BlockSpec((tn, d), tok),
        out_shape=jax.ShapeDtypeStruct((n, d), jnp.float32),
        scratch_shapes=[pltpu.VMEM((tn * HALF_KEYS, N_KEYS), jnp.int32)],
        compiler_params=_compiler_params(PEER_VMEM_LIMIT_BYTES, dimension_semantics=("arbitrary", "arbitrary")),
        name="peer_eval",
    )(xn, k1, k2, gate, u.reshape(2, n_exp // 2, d), v.reshape(2, n_exp // 2, d), h)


def kernel(x_prompt, x_sample, cache_k, cache_v, cache_kidx, state_conv_a, state_conv_b, state_pool,
           page_table, meta_tokens, w_in, conv_a_w, conv_a_b, ln_a_g, ln_a_b, conv_b_w, pool_w,
           pool_scale, w_out, norm_mix_g, norm_ffn_g, peer_wq, peer_subkeys, peer_u, peer_v, norm_final_g):
    b, seq, d = x_prompt.shape
    db, dt, _ = x_sample.shape
    depth = w_in.shape[0]
    tp = N_META + seq
    n_p, n_s = b * tp, db * dt
    n_tok = n_p + n_s
    n_pad = pl.cdiv(n_tok, PEER_TILE) * PEER_TILE
    bf = jnp.bfloat16
    past_len = page_table.shape[1] * PAGE_SIZE

    meta = jnp.broadcast_to(meta_tokens[None], (b, N_META, d))
    hp = jnp.concatenate([meta, x_prompt], axis=1).reshape(n_p, d)
    h = jnp.concatenate([hp, x_sample.reshape(n_s, d), jnp.zeros((n_pad - n_tok, d), jnp.float32)], axis=0)

    news_p, news_s = [], []
    for l in range(depth):
        lw = {'conv_a_w': conv_a_w[l], 'conv_a_b': conv_a_b[l], 'ln_a_g': ln_a_g[l], 'ln_a_b': ln_a_b[l],
              'conv_b_w': conv_b_w[l], 'pool_w': pool_w[l], 'pool_scale': pool_scale[l]}
        past = {'cache_k': cache_k, 'cache_v': cache_v, 'cache_kidx': cache_kidx, 'layer': l,
                'conv_a': state_conv_a[l], 'conv_b': state_conv_b[l], 'pool': state_pool[l],
                'page_table': page_table}
        n_mix = 5 * GROUP_W
        w_proj = jnp.concatenate([w_in[l][:, :n_mix], w_in[l][:, N_COLS - GROUP_W:], w_in[l][:, n_mix:N_COLS - GROUP_W]],
                                 axis=1).astype(bf)
        z = norm_matmul(h, norm_mix_g[l], w_proj)
        ys_p, new_p = mixer_block(z[:n_p].reshape(b, tp, N_COLS), lw, None, 0)
        ys_s, new_s = mixer_block(z[n_p:n_tok].reshape(db, dt, N_COLS), lw, past, past_len)
        ys = [jnp.concatenate([yp.reshape(n_p, GROUP_W), ysm.reshape(n_s, GROUP_W),
                               jnp.zeros((n_pad - n_tok, GROUP_W), jnp.float32)], axis=0)
              for yp, ysm in zip(ys_p, ys_s)]
        h = out_proj(h, ys, w_out[l].astype(bf))
        xn, k1, k2, gate = peer_route(h, norm_ffn_g[l], peer_wq[l].T.astype(bf),
                                      peer_subkeys[l].reshape(PEER_HEADS * 2, N_KEYS, PEER_HALF).astype(bf))
        h = peer_eval(xn, k1, k2, gate, peer_u[l].astype(bf), peer_v[l].astype(bf), h)
        news_p.append(new_p)
        news_s.append(new_s)

    y = rms_norm_rows(h, norm_final_g)
    y_prompt = y[:n_p].reshape(b, tp, d)[:, N_META:]
    y_sample = y[n_p:n_tok].reshape(db, dt, d)
    outs_p = [jnp.stack([n[i] for n in news_p]) for i in range(6)]
    outs_s = [jnp.stack([n[i] for n in news_s]) for i in range(6)]
    return (y_prompt, y_sample, *outs_p, *outs_s)
```
